```python
import math
import jax, jax.numpy as jnp
from jax import lax
import numpy as np

D_MODEL = 1024
BATCH = 2
SEQ = 8192
DEPTH = 4
DEC_BATCH = 128
DEC_SEQ = 1
PAST_LEN = 8192
PAGE_SIZE = 128

N_A = DEPTH // 2
N_B = DEPTH - N_A
CONV_W = 3
N_HEADS = 16
N_KV_HEADS = 4
HEAD_DIM = D_MODEL // N_HEADS
GROUP = N_HEADS // N_KV_HEADS
ROT_DIM = HEAD_DIM // 4
ROPE_THETA = 500000.0
WINDOW = 128
BLOCK = 128
D_FF = 2816
EPS = 1e-6
W_BUF = min(WINDOW, PAST_LEN)

kernel_name = "yoco_shortconv_swa_sink_macaron"


def _rmsnorm(x, g):
    xf = x.astype(jnp.float32)
    y = xf * lax.rsqrt(jnp.mean(xf * xf, axis=-1, keepdims=True) + EPS)
    return (y * g.astype(jnp.float32)).astype(x.dtype)


def _swiglu(h, w_gate, w_up, w_down):
    return (jax.nn.silu(h @ w_gate) * (h @ w_up)) @ w_down


def _rope(x, pos):
    half = ROT_DIM // 2
    inv_freq = ROPE_THETA ** (-jnp.arange(0, ROT_DIM, 2, dtype=jnp.float32) / ROT_DIM)
    ang = pos.astype(jnp.float32)[:, None] * inv_freq[None, :]
    cos = jnp.cos(ang)[:, None, :]
    sin = jnp.sin(ang)[:, None, :]
    xr = x[..., :ROT_DIM].astype(jnp.float32)
    x1, x2 = xr[..., :half], xr[..., half:]
    rot = jnp.concatenate([x1 * cos - x2 * sin, x2 * cos + x1 * sin], axis=-1)
    return jnp.concatenate([rot.astype(x.dtype), x[..., ROT_DIM:]], axis=-1)


def _short_conv_mixer(h, conv_state, w_in, w_conv, w_out):
    S = h.shape[1]
    b, c, u = jnp.split(h @ w_in, 3, axis=-1)
    cu = c * u
    ext = jnp.concatenate([conv_state.astype(cu.dtype), cu], axis=1)
    conv = sum(w_conv[j] * ext[:, j:j + S] for j in range(CONV_W))
    y = (b * conv) @ w_out
    return y, ext[:, -(CONV_W - 1):]


def _shared_kv(x, pos, g_kv, w_kv, g_knorm):
    B, S, _ = x.shape
    kv = _rmsnorm(x, g_kv) @ w_kv
    k, v = jnp.split(kv, 2, axis=-1)
    k = k.reshape(B, S, N_KV_HEADS, HEAD_DIM)
    v = v.reshape(B, S, N_KV_HEADS, HEAD_DIM)
    k = _rope(_rmsnorm(k, g_knorm), pos)
    return k, v


def _queries(h, pos, w_q, g_qnorm):
    B, S, _ = h.shape
    q = (h @ w_q).reshape(B, S, N_HEADS, HEAD_DIM)
    return _rope(_rmsnorm(q, g_qnorm), pos)


def _sink_weights(s, valid, sinks):
    sk = sinks.astype(jnp.float32).reshape(N_KV_HEADS, GROUP)[:, :, None, None]
    s = jnp.where(valid, s, -jnp.inf)
    m = jnp.maximum(jnp.max(s, axis=-1, keepdims=True), sk)
    p = jnp.exp(s - m)
    return p / (jnp.sum(p, axis=-1, keepdims=True) + jnp.exp(sk - m))


def _banded_window_attention(q, k, v, sinks):
    B, S, _, _ = q.shape
    nblk = S // BLOCK
    qb = q.reshape(B, nblk, BLOCK, N_KV_HEADS, GROUP, HEAD_DIM)
    kb = k.reshape(B, nblk, BLOCK, N_KV_HEADS, HEAD_DIM)
    vb = v.reshape(B, nblk, BLOCK, N_KV_HEADS, HEAD_DIM)
    kk = jnp.concatenate([jnp.concatenate([jnp.zeros_like(kb[:, :1]), kb[:, :-1]], axis=1), kb], axis=2)
    vv = jnp.concatenate([jnp.concatenate([jnp.zeros_like(vb[:, :1]), vb[:, :-1]], axis=1), vb], axis=2)
    s = jnp.einsum('bnqhgd,bnshd->bnhgqs', qb, kk).astype(jnp.float32) * (1.0 / math.sqrt(HEAD_DIM))
    qi = jnp.arange(BLOCK)[:, None] + BLOCK
    kj = jnp.arange(2 * BLOCK)[None, :]
    rel = qi - kj
    band = (rel >= 0) & (rel < WINDOW)
    kpos = (jnp.arange(nblk)[:, None, None] - 1) * BLOCK + kj[None]
    valid = (band[None] & (kpos >= 0))[None, :, None, None]
    w = _sink_weights(s, valid, sinks)
    o = jnp.einsum('bnhgqs,bnshd->bnqhgd', w.astype(v.dtype), vv)
    return o.reshape(B, S, N_HEADS * HEAD_DIM)


def _cached_window_attention(q, k_new, v_new, cache_k, cache_v, sinks):
    Bd, Sd, _, _ = q.shape
    kk = jnp.concatenate([cache_k.astype(k_new.dtype), k_new], axis=1)
    vv = jnp.concatenate([cache_v.astype(v_new.dtype), v_new], axis=1)
    qg = q.reshape(Bd, Sd, N_KV_HEADS, GROUP, HEAD_DIM)
    s = jnp.einsum('bqhgd,bshd->bhgqs', qg, kk).astype(jnp.float32) * (1.0 / math.sqrt(HEAD_DIM))
    qpos = PAST_LEN + jnp.arange(Sd)
    kpos = jnp.concatenate([PAST_LEN - W_BUF + jnp.arange(W_BUF), PAST_LEN + jnp.arange(Sd)])
    rel = qpos[:, None] - kpos[None, :]
    valid = ((rel >= 0) & (rel < WINDOW))[None, None, None]
    w = _sink_weights(s, valid, sinks)
    o = jnp.einsum('bhgqs,bshd->bqhgd', w.astype(vv.dtype), vv)
    return o.reshape(Bd, Sd, N_HEADS * HEAD_DIM)


def _trunk(x, pos, conv_in, cache_k, cache_v,
           g_ffn1, w_ffn1_gate, w_ffn1_up, w_ffn1_down, g_mix,
           g_ffn2, w_ffn2_gate, w_ffn2_up, w_ffn2_down,
           w_in_a, conv_w, w_out_a, g_kv, w_kv, g_knorm,
           w_q, g_qnorm, sinks, w_o):
    new_conv = []
    k = v = None
    for i in range(DEPTH):
        if i == N_A:
            k, v = _shared_kv(x, pos, g_kv, w_kv, g_knorm)
        x = x + 0.5 * _swiglu(_rmsnorm(x, g_ffn1[i]), w_ffn1_gate[i], w_ffn1_up[i], w_ffn1_down[i])
        h = _rmsnorm(x, g_mix[i])
        if i < N_A:
            y, st = _short_conv_mixer(h, conv_in[i], w_in_a[i], conv_w[i], w_out_a[i])
            new_conv.append(st)
        else:
            j = i - N_A
            q = _queries(h, pos, w_q[j], g_qnorm[j])
            if cache_k is None:
                o = _banded_window_attention(q, k, v, sinks[j])
            else:
                o = _cached_window_attention(q, k, v, cache_k, cache_v, sinks[j])
            y = o @ w_o[j]
        x = x + y
        x = x + 0.5 * _swiglu(_rmsnorm(x, g_ffn2[i]), w_ffn2_gate[i], w_ffn2_up[i], w_ffn2_down[i])
    if cache_k is None:
        kb, vb = k[:, -W_BUF:], v[:, -W_BUF:]
    else:
        kb = jnp.concatenate([cache_k.astype(k.dtype), k], axis=1)[:, -W_BUF:]
        vb = jnp.concatenate([cache_v.astype(v.dtype), v], axis=1)[:, -W_BUF:]
    return x, jnp.stack(new_conv), kb, vb


def setup_inputs(seed: int = 0) -> dict:
    key = jax.random.key(seed)
    ks = jax.random.split(key, 24)

    def nrm(k, shape, scale):
        return jax.random.normal(k, shape, jnp.float32) * scale

    HKV = N_KV_HEADS * HEAD_DIM
    HQ = N_HEADS * HEAD_DIM
    return {
        "x_prompt": nrm(ks[0], (BATCH, SEQ, D_MODEL), 1.0),
        "x_sample": nrm(ks[1], (DEC_BATCH, DEC_SEQ, D_MODEL), 1.0),
        "state_conv": nrm(ks[2], (N_A, DEC_BATCH, CONV_W - 1, D_MODEL), 1.0),
        "cache_k": nrm(ks[3], (DEC_BATCH, W_BUF, N_KV_HEADS, HEAD_DIM), 1.0),
        "cache_v": nrm(ks[4], (DEC_BATCH, W_BUF, N_KV_HEADS, HEAD_DIM), 1.0),
        "g_ffn1": 1.0 + nrm(ks[5], (DEPTH, D_MODEL), 0.02),
        "w_ffn1_gate": nrm(ks[6], (DEPTH, D_MODEL, D_FF), D_MODEL ** -0.5),
        "w_ffn1_up": nrm(ks[7], (DEPTH, D_MODEL, D_FF), D_MODEL ** -0.5),
        "w_ffn1_down": nrm(ks[8], (DEPTH, D_FF, D_MODEL), D_FF ** -0.5),
        "g_mix": 1.0 + nrm(ks[9], (DEPTH, D_MODEL), 0.02),
        "g_ffn2": 1.0 + nrm(ks[10], (DEPTH, D_MODEL), 0.02),
        "w_ffn2_gate": nrm(ks[11], (DEPTH, D_MODEL, D_FF), D_MODEL ** -0.5),
        "w_ffn2_up": nrm(ks[12], (DEPTH, D_MODEL, D_FF), D_MODEL ** -0.5),
        "w_ffn2_down": nrm(ks[13], (DEPTH, D_FF, D_MODEL), D_FF ** -0.5),
        "w_in_a": nrm(ks[14], (N_A, D_MODEL, 3 * D_MODEL), D_MODEL ** -0.5),
        "conv_w": nrm(ks[15], (N_A, CONV_W, D_MODEL), CONV_W ** -0.5),
        "w_out_a": nrm(ks[16], (N_A, D_MODEL, D_MODEL), D_MODEL ** -0.5),
        "g_kv": 1.0 + nrm(ks[17], (D_MODEL,), 0.02),
        "w_kv": nrm(ks[18], (D_MODEL, 2 * HKV), D_MODEL ** -0.5),
        "g_knorm": 1.0 + nrm(ks[19], (HEAD_DIM,), 0.02),
        "w_q": nrm(ks[20], (N_B, D_MODEL, HQ), D_MODEL ** -0.5),
        "g_qnorm": 1.0 + nrm(ks[21], (N_B, HEAD_DIM), 0.02),
        "sinks": nrm(ks[22], (N_B, N_HEADS), 1.0),
        "w_o": nrm(ks[23], (N_B, HQ, D_MODEL), HQ ** -0.5),
    }


def reference(x_prompt, x_sample, state_conv, cache_k, cache_v,
              g_ffn1, w_ffn1_gate, w_ffn1_up, w_ffn1_down, g_mix,
              g_ffn2, w_ffn2_gate, w_ffn2_up, w_ffn2_down,
              w_in_a, conv_w, w_out_a, g_kv, w_kv, g_knorm,
              w_q, g_qnorm, sinks, w_o):
    weights = (g_ffn1, w_ffn1_gate, w_ffn1_up, w_ffn1_down, g_mix,
               g_ffn2, w_ffn2_gate, w_ffn2_up, w_ffn2_down,
               w_in_a, conv_w, w_out_a, g_kv, w_kv, g_knorm,
               w_q, g_qnorm, sinks, w_o)
    pos_p = jnp.arange(SEQ, dtype=jnp.int32)
    conv0 = jnp.zeros((N_A, x_prompt.shape[0], CONV_W - 1, D_MODEL), x_prompt.dtype)
    y_prompt, conv_p, k_p, v_p = _trunk(x_prompt, pos_p, conv0, None, None, *weights)
    pos_s = PAST_LEN + jnp.arange(x_sample.shape[1], dtype=jnp.int32)
    y_sample, conv_s, k_s, v_s = _trunk(x_sample, pos_s, state_conv, cache_k, cache_v, *weights)
    return (y_prompt, y_sample, conv_p, k_p, v_p, conv_s, k_s, v_s)
```

```python
import functools
import math

import jax
import jax.numpy as jnp
from jax import lax
from jax.experimental import pallas as pl
from jax.experimental.pallas import tpu as pltpu

D_MODEL = 1024
D_FF = 2816
N_HEADS = 16
N_KV_HEADS = 4
GROUP = N_HEADS // N_KV_HEADS
HEAD_DIM = 64
KV_DIM = N_KV_HEADS * HEAD_DIM
ROT_DIM = HEAD_DIM // 4
ROPE_THETA = 500000.0
WINDOW = 128
EPS = 1e-6
CONV_W = 3

LANES = 128
SUBLANES = 8
VMEM_LIMIT = 56 * 1024 * 1024

FFN_CHUNK = 256
TM_FFN = 512
TM_MIX = 512
SAMPLE_CHUNK = 32

F32 = jnp.float32
BF16 = jnp.bfloat16


def _const_spec(shape):
    return pl.BlockSpec(shape, lambda *_: (0,) * len(shape), pipeline_mode=pl.Buffered(1))


def _params(n_axes):
    return pltpu.CompilerParams(dimension_semantics=("arbitrary",) * n_axes,
                                vmem_limit_bytes=VMEM_LIMIT)


def _dot(a, b):
    return jnp.dot(a, b, preferred_element_type=F32)


def _dot_nt(a, b):
    return lax.dot_general(a, b, (((1,), (1,)), ((), ())), preferred_element_type=F32)


def _rms(x, g):
    ms = jnp.mean(x * x, axis=-1, keepdims=True)
    return x * lax.rsqrt(ms + EPS) * g


def _head_mean_matrix():
    r = lax.broadcasted_iota(jnp.int32, (LANES, LANES), 0) // HEAD_DIM
    c = lax.broadcasted_iota(jnp.int32, (LANES, LANES), 1) // HEAD_DIM
    return jnp.where(r == c, 1.0 / HEAD_DIM, 0.0).astype(BF16)


def _head_norm_rope(x, gain, cos, sin_lo, sin_hi):
    bd = _head_mean_matrix()
    outs = []
    for j in range(x.shape[1] // LANES):
        xs = x[:, LANES * j:LANES * (j + 1)]
        sq = xs * xs
        hi = sq.astype(BF16)
        lo = (sq - hi.astype(F32)).astype(BF16)
        ms = _dot(hi, bd) + _dot(lo, bd)
        xn = xs * lax.rsqrt(ms + EPS) * gain[:, LANES * j:LANES * (j + 1)]
        outs.append(xn * cos + pltpu.roll(xn, ROT_DIM // 2, 1) * sin_hi
                    + pltpu.roll(xn, LANES - ROT_DIM // 2, 1) * sin_lo)
    return jnp.concatenate(outs, axis=1)


def _ffn_body(x_ref, g_ref, wg_ref, wu_ref, wd_ref, o_ref):
    x = x_ref[...]
    h = _rms(x, g_ref[...]).astype(BF16)
    acc = jnp.zeros_like(x)
    for c in range(D_FF // FFN_CHUNK):
        sl = slice(c * FFN_CHUNK, (c + 1) * FFN_CHUNK)
        gate = _dot(h, wg_ref[:, sl])
        up = _dot(h, wu_ref[:, sl])
        a = (gate * jax.nn.sigmoid(gate) * up).astype(BF16)
        acc = acc + _dot(a, wd_ref[sl, :])
    o_ref[...] = x + 0.5 * acc


def _ffn(x, g, wg, wu, wd, tm):
    m = x.shape[0]
    row = pl.BlockSpec((tm, D_MODEL), lambda i: (i, 0))
    return pl.pallas_call(
        _ffn_body, grid=(m // tm,),
        in_specs=[row, _const_spec((1, D_MODEL)), _const_spec((D_MODEL, D_FF)),
                  _const_spec((D_MODEL, D_FF)), _const_spec((D_FF, D_MODEL))],
        out_specs=row,
        out_shape=jax.ShapeDtypeStruct((m, D_MODEL), F32),
        compiler_params=_params(1), name="ffn")(x, g, wg, wu, wd)


def _conv_prompt_body(x_ref, st_ref, g_ref, win_ref, cw_ref, wout_ref, o_ref, nst_ref, ext_ref):
    tm = x_ref.shape[0]
    lead = SUBLANES - (CONV_W - 1)

    @pl.when(pl.program_id(1) == 0)
    def _():
        ext_ref[lead:SUBLANES, :] = st_ref[...]

    x = x_ref[...]
    h = _rms(x, g_ref[...]).astype(BF16)
    bcu = _dot(h, win_ref[...])
    b = bcu[:, :D_MODEL]
    cu = bcu[:, D_MODEL:2 * D_MODEL] * bcu[:, 2 * D_MODEL:]
    ext_ref[SUBLANES:SUBLANES + tm, :] = cu
    cw = cw_ref[...]
    conv = (cw[0:1] * ext_ref[lead:lead + tm, :] + cw[1:2] * ext_ref[lead + 1:lead + 1 + tm, :]
            + cw[2:3] * cu)
    y = _dot((b * conv).astype(BF16), wout_ref[...])
    o_ref[...] = x + y
    tail = ext_ref[lead + tm:SUBLANES + tm, :]
    ext_ref[lead:SUBLANES, :] = tail
    nst_ref[...] = tail


def _conv_prompt(x, state, g, w_in, cw, w_out, tm):
    nb, s, _ = x.shape
    row = pl.BlockSpec((None, tm, D_MODEL), lambda b, t: (b, t, 0))
    st = pl.BlockSpec((None, CONV_W - 1, D_MODEL), lambda b, t: (b, 0, 0))
    return pl.pallas_call(
        _conv_prompt_body, grid=(nb, s // tm),
        in_specs=[row, st, _const_spec((1, D_MODEL)), _const_spec((D_MODEL, 3 * D_MODEL)),
                  _const_spec((CONV_W, D_MODEL)), _const_spec((D_MODEL, D_MODEL))],
        out_specs=[row, st],
        out_shape=[jax.ShapeDtypeStruct(x.shape, F32),
                   jax.ShapeDtypeStruct((nb, CONV_W - 1, D_MODEL), F32)],
        scratch_shapes=[pltpu.VMEM((tm + SUBLANES, D_MODEL), F32)],
        compiler_params=_params(2), name="conv_prompt")(x, state, g, w_in, cw, w_out)


def _conv_sample_body(x_ref, st_ref, g_ref, win_ref, cw_ref, wout_ref, o_ref, nst_ref):
    x = x_ref[...]
    h = _rms(x, g_ref[...]).astype(BF16)
    bcu = _dot(h, win_ref[...])
    b = bcu[:, :D_MODEL]
    cu = bcu[:, D_MODEL:2 * D_MODEL] * bcu[:, 2 * D_MODEL:]
    cw = cw_ref[...]
    s1 = st_ref[:, D_MODEL:]
    conv = cw[0:1] * st_ref[:, :D_MODEL] + cw[1:2] * s1 + cw[2:3] * cu
    o_ref[...] = x + _dot((b * conv).astype(BF16), wout_ref[...])
    nst_ref[:, :D_MODEL] = s1
    nst_ref[:, D_MODEL:] = cu


def _conv_sample(x, state, g, w_in, cw, w_out):
    m = x.shape[0]
    return pl.pallas_call(
        _conv_sample_body, grid=(1,),
        in_specs=[_const_spec((m, D_MODEL)), _const_spec((m, 2 * D_MODEL)),
                  _const_spec((1, D_MODEL)), _const_spec((D_MODEL, 3 * D_MODEL)),
                  _const_spec((CONV_W, D_MODEL)), _const_spec((D_MODEL, D_MODEL))],
        out_specs=[pl.BlockSpec((m, D_MODEL), lambda i: (0, 0)),
                   pl.BlockSpec((m, 2 * D_MODEL), lambda i: (0, 0))],
        out_shape=[jax.ShapeDtypeStruct((m, D_MODEL), F32),
                   jax.ShapeDtypeStruct((m, 2 * D_MODEL), F32)],
        compiler_params=_params(1), name="conv_sample")(x, state, g, w_in, cw, w_out)


def _kv_rows(x, g, w, gk, cos, sin_lo, sin_hi):
    kv = _dot(_rms(x, g).astype(BF16), w)
    k = _head_norm_rope(kv[:, :KV_DIM], gk, cos, sin_lo, sin_hi)
    return k, kv[:, KV_DIM:]


def _kv_prompt_body(x_ref, g_ref, w_ref, gk_ref, cos_ref, slo_ref, shi_ref, k_ref, v_ref):
    k, v = _kv_rows(x_ref[...], g_ref[...], w_ref[...], gk_ref[...],
                    cos_ref[...], slo_ref[...], shi_ref[...])
    k_ref[...] = k
    v_ref[...] = v


def _kv_prompt(x, g, w, gk, tables, tm):
    nb, s, _ = x.shape
    row = pl.BlockSpec((None, tm, D_MODEL), lambda b, t: (b, t, 0))
    tab = pl.BlockSpec((tm, LANES), lambda b, t: (t, 0))
    out = pl.BlockSpec((None, tm, KV_DIM), lambda b, t: (b, t, 0))
    return pl.pallas_call(
        _kv_prompt_body, grid=(nb, s // tm),
        in_specs=[row, _const_spec((1, D_MODEL)), _const_spec((D_MODEL, 2 * KV_DIM)),
                  _const_spec((1, KV_DIM)), tab, tab, tab],
        out_specs=[out, out],
        out_shape=[jax.ShapeDtypeStruct((nb, s, KV_DIM), F32)] * 2,
        compiler_params=_params(2), name="kv_prompt")(x, g, w, gk, *tables)


def _kv_sample_body(x_ref, g_ref, w_ref, gk_ref, cos_ref, slo_ref, shi_ref, ck_ref, cv_ref,
                    k_ref, v_ref, nk_ref, nv_ref):
    k, v = _kv_rows(x_ref[...], g_ref[...], w_ref[...], gk_ref[...],
                    cos_ref[...], slo_ref[...], shi_ref[...])
    k_ref[...] = k
    v_ref[...] = v
    w_buf = ck_ref.shape[1]

    def shift(b, carry):
        nk_ref[b, 0:w_buf - 1, :] = ck_ref[b, 1:w_buf, :]
        nv_ref[b, 0:w_buf - 1, :] = cv_ref[b, 1:w_buf, :]
        nk_ref[b, w_buf - 1:w_buf, :] = k_ref[pl.ds(b, 1), :]
        nv_ref[b, w_buf - 1:w_buf, :] = v_ref[pl.ds(b, 1), :]
        return carry

    lax.fori_loop(0, x_ref.shape[0], shift, 0)


def _kv_sample(x, g, w, gk, tables, ck, cv, chunk):
    m = x.shape[0]
    w_buf = ck.shape[1]
    row = pl.BlockSpec((chunk, D_MODEL), lambda i: (i, 0))
    tab = pl.BlockSpec((chunk, LANES), lambda i: (i, 0))
    new = pl.BlockSpec((chunk, KV_DIM), lambda i: (i, 0))
    cache = pl.BlockSpec((chunk, w_buf, KV_DIM), lambda i: (i, 0, 0))
    return pl.pallas_call(
        _kv_sample_body, grid=(m // chunk,),
        in_specs=[row, _const_spec((1, D_MODEL)), _const_spec((D_MODEL, 2 * KV_DIM)),
                  _const_spec((1, KV_DIM)), tab, tab, tab, cache, cache],
        out_specs=[new, new, cache, cache],
        out_shape=[jax.ShapeDtypeStruct((m, KV_DIM), F32)] * 2
        + [jax.ShapeDtypeStruct(ck.shape, F32)] * 2,
        compiler_params=_params(1), name="kv_sample")(x, g, w, gk, *tables, ck, cv)


def _queries(x, g, wq, gq, cos, sin_lo, sin_hi):
    q = _dot(_rms(x, g).astype(BF16), wq)
    return _head_norm_rope(q, gq, cos, sin_lo, sin_hi) * (1.0 / math.sqrt(HEAD_DIM))


def _attn_prompt_body(sinks_ref, x_ref, kc_ref, kp_ref, vc_ref, vp_ref, g_ref, wq_ref, gq_ref,
                      cos_ref, slo_ref, shi_ref, wo_ref, o_ref, q_scr, o_scr):
    tm = x_ref.shape[0]
    first_tile = pl.program_id(1) == 0
    x = x_ref[...]
    q_scr[...] = _queries(x, g_ref[...], wq_ref[...], gq_ref[...],
                          cos_ref[...], slo_ref[...], shi_ref[...]).astype(BF16)
    kcat = jnp.concatenate([kp_ref[...], kc_ref[...]], axis=0).astype(BF16)
    vcat = jnp.concatenate([vp_ref[...], vc_ref[...]], axis=0).astype(BF16)

    seg = lax.broadcasted_iota(jnp.int32, (1, KV_DIM), 1) // HEAD_DIM
    qi = lax.broadcasted_iota(jnp.int32, (WINDOW, 2 * WINDOW), 0)
    kj = lax.broadcasted_iota(jnp.int32, (WINDOW, 2 * WINDOW), 1)
    rel = qi + WINDOW - kj
    band = (rel >= 0) & (rel < WINDOW)
    band0 = band & ((kj >= WINDOW) | jnp.logical_not(first_tile))

    for i in range(tm // WINDOW):
        rows = slice(WINDOW * i, WINDOW * (i + 1))
        kw = kcat[WINDOW * i:WINDOW * (i + 2)]
        vw = vcat[WINDOW * i:WINDOW * (i + 2)]
        valid = band0 if i == 0 else band
        valid4 = jnp.concatenate([valid] * N_KV_HEADS, axis=0)
        for g in range(GROUP):
            lanes = slice(KV_DIM * g, KV_DIM * (g + 1))
            slab = q_scr[rows, lanes]
            lhs = jnp.concatenate(
                [jnp.where(seg == h, slab, jnp.zeros_like(slab)) for h in range(N_KV_HEADS)],
                axis=0)
            s = jnp.where(valid4, _dot_nt(lhs, kw), -jnp.inf)
            sink = jnp.concatenate(
                [jnp.full((WINDOW, 1), sinks_ref[N_KV_HEADS * g + h], F32)
                 for h in range(N_KV_HEADS)], axis=0)
            m = jnp.maximum(jnp.max(s, axis=-1, keepdims=True), sink)
            p = jnp.exp(s - m)
            den = jnp.sum(p, axis=-1, keepdims=True) + jnp.exp(sink - m)
            r = _dot(p.astype(BF16), vw) * (1.0 / den)
            o_slab = jnp.zeros((WINDOW, KV_DIM), F32)
            for h in range(N_KV_HEADS):
                o_slab = o_slab + jnp.where(seg == h, r[WINDOW * h:WINDOW * (h + 1)], 0.0)
            o_scr[rows, lanes] = o_slab.astype(BF16)

    o_ref[...] = x + _dot(o_scr[...], wo_ref[...])


def _attn_prompt(x, k, v, g, wq, gq, tables, sinks, wo, tm):
    nb, s, _ = x.shape
    per_tile = tm // WINDOW
    row = pl.BlockSpec((None, tm, D_MODEL), lambda b, t: (b, t, 0))
    cur = pl.BlockSpec((None, tm, KV_DIM), lambda b, t: (b, t, 0))
    prev = pl.BlockSpec((None, WINDOW, KV_DIM),
                        lambda b, t: (b, jnp.maximum(t * per_tile - 1, 0), 0))
    tab = pl.BlockSpec((tm, LANES), lambda b, t: (t, 0))
    return pl.pallas_call(
        _attn_prompt_body, grid=(nb, s // tm),
        in_specs=[pl.BlockSpec(memory_space=pltpu.SMEM), row, cur, prev, cur, prev,
                  _const_spec((1, D_MODEL)), _const_spec((D_MODEL, D_MODEL)),
                  _const_spec((1, D_MODEL)), tab, tab, tab, _const_spec((D_MODEL, D_MODEL))],
        out_specs=row,
        out_shape=jax.ShapeDtypeStruct(x.shape, F32),
        scratch_shapes=[pltpu.VMEM((tm, D_MODEL), BF16), pltpu.VMEM((tm, D_MODEL), BF16)],
        compiler_params=_params(2), name="attn_prompt")(
            sinks, x, k, k, v, v, g, wq, gq, *tables, wo)


def _attn_sample_body(sinks_ref, x_ref, ck_ref, cv_ref, kn_ref, vn_ref, g_ref, wq_ref, gq_ref,
                      cos_ref, slo_ref, shi_ref, wo_ref, o_ref, q_scr, o_scr):
    x = x_ref[...]
    q_scr[...] = _queries(x, g_ref[...], wq_ref[...], gq_ref[...],
                          cos_ref[...], slo_ref[...], shi_ref[...])
    w_buf = ck_ref.shape[1]
    seg_rows = (lax.broadcasted_iota(jnp.int32, (N_KV_HEADS, KV_DIM), 1) // HEAD_DIM
                == lax.broadcasted_iota(jnp.int32, (N_KV_HEADS, KV_DIM), 0))
    in_window = lax.broadcasted_iota(jnp.int32, (1, w_buf), 1) >= 1
    sink = jnp.concatenate([jnp.full((1, 1), sinks_ref[i], F32) for i in range(N_HEADS)], axis=0)

    def one(b, carry):
        kb = ck_ref[b].astype(BF16)
        vb = cv_ref[b].astype(BF16)
        qrow = q_scr[pl.ds(b, 1), :]
        qrows = jnp.concatenate(
            [jnp.where(seg_rows, jnp.broadcast_to(qrow[:, KV_DIM * g:KV_DIM * (g + 1)],
                                                  (N_KV_HEADS, KV_DIM)), 0.0)
             for g in range(GROUP)], axis=0)
        s = jnp.where(in_window, _dot_nt(qrows.astype(BF16), kb), -jnp.inf)
        s_new = jnp.sum(qrows * kn_ref[pl.ds(b, 1), :], axis=-1, keepdims=True)
        m = jnp.maximum(jnp.maximum(jnp.max(s, axis=-1, keepdims=True), s_new), sink)
        p = jnp.exp(s - m)
        p_new = jnp.exp(s_new - m)
        den = jnp.sum(p, axis=-1, keepdims=True) + p_new + jnp.exp(sink - m)
        r = (_dot(p.astype(BF16), vb) + p_new * vn_ref[pl.ds(b, 1), :]) * (1.0 / den)
        o_scr[pl.ds(b, 1), :] = jnp.concatenate(
            [jnp.sum(jnp.where(seg_rows, r[N_KV_HEADS * g:N_KV_HEADS * (g + 1)], 0.0),
                     axis=0, keepdims=True) for g in range(GROUP)], axis=1)
        return carry

    lax.fori_loop(0, x_ref.shape[0], one, 0)
    o_ref[...] = x + _dot(o_scr[...].astype(BF16), wo_ref[...])


def _attn_sample(x, ck, cv, kn, vn, g, wq, gq, tables, sinks, wo, chunk):
    m = x.shape[0]
    w_buf = ck.shape[1]
    row = pl.BlockSpec((chunk, D_MODEL), lambda i: (i, 0))
    tab = pl.BlockSpec((chunk, LANES), lambda i: (i, 0))
    new = pl.BlockSpec((chunk, KV_DIM), lambda i: (i, 0))
    cache = pl.BlockSpec((chunk, w_buf, KV_DIM), lambda i: (i, 0, 0))
    return pl.pallas_call(
        _attn_sample_body, grid=(m // chunk,),
        in_specs=[pl.BlockSpec(memory_space=pltpu.SMEM), row, cache, cache, new, new,
                  _const_spec((1, D_MODEL)), _const_spec((D_MODEL, D_MODEL)),
                  _const_spec((1, D_MODEL)), tab, tab, tab, _const_spec((D_MODEL, D_MODEL))],
        out_specs=row,
        out_shape=jax.ShapeDtypeStruct(x.shape, F32),
        scratch_shapes=[pltpu.VMEM((chunk, D_MODEL), F32), pltpu.VMEM((chunk, D_MODEL), F32)],
        compiler_params=_params(1), name="attn_sample")(
            sinks, x, ck, cv, kn, vn, g, wq, gq, *tables, wo)


def _rope_tables(pos):
    half = ROT_DIM // 2
    inv_freq = ROPE_THETA ** (-jnp.arange(0, ROT_DIM, 2, dtype=F32) / ROT_DIM)
    ang = pos.astype(F32)[:, None] * inv_freq[None, :]
    cos, sin = jnp.cos(ang), jnp.sin(ang)
    n = pos.shape[0]
    rest = HEAD_DIM - ROT_DIM
    cos_h = jnp.concatenate([cos, cos, jnp.ones((n, rest), F32)], axis=1)
    sin_lo = jnp.concatenate([-sin, jnp.zeros((n, half + rest), F32)], axis=1)
    sin_hi = jnp.concatenate([jnp.zeros((n, half), F32), sin, jnp.zeros((n, rest), F32)], axis=1)
    reps = LANES // HEAD_DIM
    return tuple(jnp.tile(t, (1, reps)) for t in (cos_h, sin_lo, sin_hi))


def _head_permutation():
    g, h, d = jnp.meshgrid(jnp.arange(GROUP), jnp.arange(N_KV_HEADS), jnp.arange(HEAD_DIM),
                           indexing="ij")
    return (HEAD_DIM * (GROUP * h + g) + d).reshape(-1)


def kernel(x_prompt, x_sample, state_conv, cache_k, cache_v, g_ffn1, w_ffn1_gate, w_ffn1_up,
           w_ffn1_down, g_mix, g_ffn2, w_ffn2_gate, w_ffn2_up, w_ffn2_down, w_in_a, conv_w,
           w_out_a, g_kv, w_kv, g_knorm, w_q, g_qnorm, sinks, w_o):
    nb, seq, _ = x_prompt.shape
    nd, dec_seq, _ = x_sample.shape
    assert dec_seq == 1
    depth = g_ffn1.shape[0]
    n_a = w_in_a.shape[0]
    past_len = seq
    w_buf = cache_k.shape[1]

    perm = _head_permutation()
    tab_p = _rope_tables(jnp.arange(seq, dtype=jnp.int32))
    tab_s = _rope_tables(jnp.full((nd,), past_len, jnp.int32))
    gk = jnp.tile(g_knorm, N_KV_HEADS)[None]

    xp = x_prompt
    xs = x_sample.reshape(nd, D_MODEL)
    ck = cache_k.reshape(nd, w_buf, KV_DIM)
    cv = cache_v.reshape(nd, w_buf, KV_DIM)
    conv_p, conv_s = [], []
    kp = vp = kn = vn = new_ck = new_cv = None

    def ffn_both(xp, xs, g, wg, wu, wd):
        g, wg, wu, wd = g[None], wg.astype(BF16), wu.astype(BF16), wd.astype(BF16)
        xp = _ffn(xp.reshape(nb * seq, D_MODEL), g, wg, wu, wd, TM_FFN).reshape(nb, seq, D_MODEL)
        return xp, _ffn(xs, g, wg, wu, wd, nd)

    for i in range(depth):
        if i == n_a:
            wkv = w_kv.astype(BF16)
            kp, vp = _kv_prompt(xp, g_kv[None], wkv, gk, tab_p, TM_MIX)
            kn, vn, new_ck, new_cv = _kv_sample(xs, g_kv[None], wkv, gk, tab_s, ck, cv,
                                                SAMPLE_CHUNK)
        xp, xs = ffn_both(xp, xs, g_ffn1[i], w_ffn1_gate[i], w_ffn1_up[i], w_ffn1_down[i])
        gm = g_mix[i][None]
        if i < n_a:
            w_in, w_out = w_in_a[i].astype(BF16), w_out_a[i].astype(BF16)
            xp, st = _conv_prompt(xp, jnp.zeros((nb, CONV_W - 1, D_MODEL), F32), gm, w_in,
                                  conv_w[i], w_out, TM_MIX)
            conv_p.append(st)
            xs, st = _conv_sample(xs, state_conv[i].reshape(nd, (CONV_W - 1) * D_MODEL), gm,
                                  w_in, conv_w[i], w_out)
            conv_s.append(st.reshape(nd, CONV_W - 1, D_MODEL))
        else:
            j = i - n_a
            wq = w_q[j][:, perm].astype(BF16)
            wo = w_o[j][perm, :].astype(BF16)
            gq = jnp.tile(g_qnorm[j], N_HEADS)[None]
            sk = sinks[j].reshape(N_KV_HEADS, GROUP).T.reshape(-1)
            xp = _attn_prompt(xp, kp, vp, gm, wq, gq, tab_p, sk, wo, TM_MIX)
            xs = _attn_sample(xs, ck, cv, kn, vn, gm, wq, gq, tab_s, sk, wo, SAMPLE_CHUNK)
        xp, xs = ffn_both(xp, xs, g_ffn2[i], w_ffn2_gate[i], w_ffn2_up[i], w_ffn2_down[i])

    kv_shape = (w_buf, N_KV_HEADS, HEAD_DIM)
    return (xp, xs.reshape(nd, 1, D_MODEL), jnp.stack(conv_p),
            kp[:, seq - w_buf:].reshape(nb, *kv_shape), vp[:, seq - w_buf:].reshape(nb, *kv_shape),
            jnp.stack(conv_s), new_ck.reshape(nd, *kv_shape), new_cv.reshape(nd, *kv_shape))
```

```python
import functools
import math

import jax
import jax.numpy as jnp
from jax import lax
from jax.experimental import pallas as pl
from jax.experimental.pallas import tpu as pltpu

D_MODEL = 1024
D_FF = 2816
N_HEADS = 16
N_KV_HEADS = 4
GROUP = N_HEADS // N_KV_HEADS
HEAD_DIM = 64
KV_DIM = N_KV_HEADS * HEAD_DIM
ROT_DIM = HEAD_DIM // 4
ROPE_THETA = 500000.0
WINDOW = 128
EPS = 1e-6
CONV_W = 3

LANES = 128
SUBLANES = 8
VMEM_LIMIT = 56 * 1024 * 1024

FFN_CHUNK = 256
TM_FFN = 512
TM_MIX = 512
SAMPLE_CHUNK = 32

F32 = jnp.float32
BF16 = jnp.bfloat16


def _const_spec(shape):
    return pl.BlockSpec(shape, lambda *_: (0,) * len(shape), pipeline_mode=pl.Buffered(1))


def _params(n_axes):
    return pltpu.CompilerParams(dimension_semantics=("arbitrary",) * n_axes,
                                vmem_limit_bytes=VMEM_LIMIT)


def _dot(a, b):
    return jnp.dot(a, b, preferred_element_type=F32)


def _dot_nt(a, b):
    return lax.dot_general(a, b, (((1,), (1,)), ((), ())), preferred_element_type=F32)


def _rms(x, g):
    ms = jnp.mean(x * x, axis=-1, keepdims=True)
    return x * lax.rsqrt(ms + EPS) * g


def _head_mean_matrix():
    r = lax.broadcasted_iota(jnp.int32, (LANES, LANES), 0) // HEAD_DIM
    c = lax.broadcasted_iota(jnp.int32, (LANES, LANES), 1) // HEAD_DIM
    return jnp.where(r == c, 1.0 / HEAD_DIM, 0.0).astype(BF16)


def _head_norm_rope(x, gain, cos, sin_lo, sin_hi):
    bd = _head_mean_matrix()
    outs = []
    for j in range(x.shape[1] // LANES):
        xs = x[:, LANES * j:LANES * (j + 1)]
        sq = xs * xs
        hi = sq.astype(BF16)
        lo = (sq - hi.astype(F32)).astype(BF16)
        ms = _dot(hi, bd) + _dot(lo, bd)
        xn = xs * lax.rsqrt(ms + EPS) * gain[:, LANES * j:LANES * (j + 1)]
        outs.append(xn * cos + pltpu.roll(xn, ROT_DIM // 2, 1) * sin_hi
                    + pltpu.roll(xn, LANES - ROT_DIM // 2, 1) * sin_lo)
    return jnp.concatenate(outs, axis=1)


def _ffn_body(x_ref, g_ref, wg_ref, wu_ref, wd_ref, o_ref):
    x = x_ref[...]
    h = _rms(x, g_ref[...]).astype(BF16)
    acc = jnp.zeros_like(x)
    for c in range(D_FF // FFN_CHUNK):
        sl = slice(c * FFN_CHUNK, (c + 1) * FFN_CHUNK)
        gate = _dot(h, wg_ref[:, sl])
        up = _dot(h, wu_ref[:, sl])
        a = (gate * jax.nn.sigmoid(gate) * up).astype(BF16)
        acc = acc + _dot(a, wd_ref[sl, :])
    o_ref[...] = x + 0.5 * acc


def _ffn(x, g, wg, wu, wd, tm):
    m = x.shape[0]
    row = pl.BlockSpec((tm, D_MODEL), lambda i: (i, 0))
    return pl.pallas_call(
        _ffn_body, grid=(m // tm,),
        in_specs=[row, _const_spec((1, D_MODEL)), _const_spec((D_MODEL, D_FF)),
                  _const_spec((D_MODEL, D_FF)), _const_spec((D_FF, D_MODEL))],
        out_specs=row,
        out_shape=jax.ShapeDtypeStruct((m, D_MODEL), F32),
        compiler_params=_params(1), name="ffn")(x, g, wg, wu, wd)


def _conv_prompt_body(x_ref, st_ref, g_ref, win_ref, cw_ref, wout_ref, o_ref, nst_ref, ext_ref):
    tm = x_ref.shape[0]
    lead = SUBLANES - (CONV_W - 1)

    @pl.when(pl.program_id(1) == 0)
    def _():
        ext_ref[lead:SUBLANES, :] = st_ref[...]

    x = x_ref[...]
    h = _rms(x, g_ref[...]).astype(BF16)
    bcu = _dot(h, win_ref[...])
    b = bcu[:, :D_MODEL]
    cu = bcu[:, D_MODEL:2 * D_MODEL] * bcu[:, 2 * D_MODEL:]
    ext_ref[SUBLANES:SUBLANES + tm, :] = cu
    cw = cw_ref[...]
    conv = (cw[0:1] * ext_ref[lead:lead + tm, :] + cw[1:2] * ext_ref[lead + 1:lead + 1 + tm, :]
            + cw[2:3] * cu)
    y = _dot((b * conv).astype(BF16), wout_ref[...])
    o_ref[...] = x + y
    tail = ext_ref[lead + tm:SUBLANES + tm, :]
    ext_ref[lead:SUBLANES, :] = tail
    nst_ref[...] = tail


def _conv_prompt(x, state, g, w_in, cw, w_out, tm):
    nb, s, _ = x.shape
    row = pl.BlockSpec((None, tm, D_MODEL), lambda b, t: (b, t, 0))
    st = pl.BlockSpec((None, CONV_W - 1, D_MODEL), lambda b, t: (b, 0, 0))
    return pl.pallas_call(
        _conv_prompt_body, grid=(nb, s // tm),
        in_specs=[row, st, _const_spec((1, D_MODEL)), _const_spec((D_MODEL, 3 * D_MODEL)),
                  _const_spec((CONV_W, D_MODEL)), _const_spec((D_MODEL, D_MODEL))],
        out_specs=[row, st],
        out_shape=[jax.ShapeDtypeStruct(x.shape, F32),
                   jax.ShapeDtypeStruct((nb, CONV_W - 1, D_MODEL), F32)],
        scratch_shapes=[pltpu.VMEM((tm + SUBLANES, D_MODEL), F32)],
        compiler_params=_params(2), name="conv_prompt")(x, state, g, w_in, cw, w_out)


def _conv_sample_body(x_ref, st_ref, g_ref, win_ref, cw_ref, wout_ref, o_ref, nst_ref):
    x = x_ref[...]
    h = _rms(x, g_ref[...]).astype(BF16)
    bcu = _dot(h, win_ref[...])
    b = bcu[:, :D_MODEL]
    cu = bcu[:, D_MODEL:2 * D_MODEL] * bcu[:, 2 * D_MODEL:]
    cw = cw_ref[...]
    s1 = st_ref[:, D_MODEL:]
    conv = cw[0:1] * st_ref[:, :D_MODEL] + cw[1:2] * s1 + cw[2:3] * cu
    o_ref[...] = x + _dot((b * conv).astype(BF16), wout_ref[...])
    nst_ref[:, :D_MODEL] = s1
    nst_ref[:, D_MODEL:] = cu


def _conv_sample(x, state, g, w_in, cw, w_out):
    m = x.shape[0]
    return pl.pallas_call(
        _conv_sample_body, grid=(1,),
        in_specs=[_const_spec((m, D_MODEL)), _const_spec((m, 2 * D_MODEL)),
                  _const_spec((1, D_MODEL)), _const_spec((D_MODEL, 3 * D_MODEL)),
                  _const_spec((CONV_W, D_MODEL)), _const_spec((D_MODEL, D_MODEL))],
        out_specs=[pl.BlockSpec((m, D_MODEL), lambda i: (0, 0)),
                   pl.BlockSpec((m, 2 * D_MODEL), lambda i: (0, 0))],
        out_shape=[jax.ShapeDtypeStruct((m, D_MODEL), F32),
                   jax.ShapeDtypeStruct((m, 2 * D_MODEL), F32)],
        compiler_params=_params(1), name="conv_sample")(x, state, g, w_in, cw, w_out)


def _kv_rows(x, g, w, gk, cos, sin_lo, sin_hi):
    kv = _dot(_rms(x, g).astype(BF16), w)
    k = _head_norm_rope(kv[:, :KV_DIM], gk, cos, sin_lo, sin_hi)
    return k, kv[:, KV_DIM:]


def _kv_prompt_body(x_ref, g_ref, w_ref, gk_ref, cos_ref, slo_ref, shi_ref, k_ref, v_ref, vt_ref):
    k, v = _kv_rows(x_ref[...], g_ref[...], w_ref[...], gk_ref[...],
                    cos_ref[...], slo_ref[...], shi_ref[...])
    k_ref[...] = k
    v_ref[...] = v
    vt_ref[...] = v.T


def _kv_prompt(x, g, w, gk, tables, tm):
    nb, s, _ = x.shape
    row = pl.BlockSpec((None, tm, D_MODEL), lambda b, t: (b, t, 0))
    tab = pl.BlockSpec((tm, LANES), lambda b, t: (t, 0))
    out = pl.BlockSpec((None, tm, KV_DIM), lambda b, t: (b, t, 0))
    out_t = pl.BlockSpec((None, KV_DIM, tm), lambda b, t: (b, 0, t))
    return pl.pallas_call(
        _kv_prompt_body, grid=(nb, s // tm),
        in_specs=[row, _const_spec((1, D_MODEL)), _const_spec((D_MODEL, 2 * KV_DIM)),
                  _const_spec((1, KV_DIM)), tab, tab, tab],
        out_specs=[out, out, out_t],
        out_shape=[jax.ShapeDtypeStruct((nb, s, KV_DIM), F32)] * 2
        + [jax.ShapeDtypeStruct((nb, KV_DIM, s), F32)],
        compiler_params=_params(2), name="kv_prompt")(x, g, w, gk, *tables)


def _kv_sample_body(x_ref, g_ref, w_ref, gk_ref, cos_ref, slo_ref, shi_ref, ck_ref, cv_ref,
                    k_ref, v_ref, nk_ref, nv_ref):
    k, v = _kv_rows(x_ref[...], g_ref[...], w_ref[...], gk_ref[...],
                    cos_ref[...], slo_ref[...], shi_ref[...])
    k_ref[...] = k
    v_ref[...] = v
    w_buf = ck_ref.shape[1]

    def shift(b, carry):
        nk_ref[b, 0:w_buf - 1, :] = ck_ref[b, 1:w_buf, :]
        nv_ref[b, 0:w_buf - 1, :] = cv_ref[b, 1:w_buf, :]
        nk_ref[b, w_buf - 1:w_buf, :] = k_ref[pl.ds(b, 1), :]
        nv_ref[b, w_buf - 1:w_buf, :] = v_ref[pl.ds(b, 1), :]
        return carry

    lax.fori_loop(0, x_ref.shape[0], shift, 0)


def _kv_sample(x, g, w, gk, tables, ck, cv, chunk):
    m = x.shape[0]
    w_buf = ck.shape[1]
    row = pl.BlockSpec((chunk, D_MODEL), lambda i: (i, 0))
    tab = pl.BlockSpec((chunk, LANES), lambda i: (i, 0))
    new = pl.BlockSpec((chunk, KV_DIM), lambda i: (i, 0))
    cache = pl.BlockSpec((chunk, w_buf, KV_DIM), lambda i: (i, 0, 0))
    return pl.pallas_call(
        _kv_sample_body, grid=(m // chunk,),
        in_specs=[row, _const_spec((1, D_MODEL)), _const_spec((D_MODEL, 2 * KV_DIM)),
                  _const_spec((1, KV_DIM)), tab, tab, tab, cache, cache],
        out_specs=[new, new, cache, cache],
        out_shape=[jax.ShapeDtypeStruct((m, KV_DIM), F32)] * 2
        + [jax.ShapeDtypeStruct(ck.shape, F32)] * 2,
        compiler_params=_params(1), name="kv_sample")(x, g, w, gk, *tables, ck, cv)


def _queries(x, g, wq, gq, cos, sin_lo, sin_hi):
    q = _dot(_rms(x, g).astype(BF16), wq)
    return _head_norm_rope(q, gq, cos, sin_lo, sin_hi) * (1.0 / math.sqrt(HEAD_DIM))


def _attn_prompt_body(sinks_ref, x_ref, kc_ref, kp_ref, vtc_ref, vtp_ref, g_ref, wq_ref, gq_ref,
                      cos_ref, slo_ref, shi_ref, wo_ref, o_ref, q_scr, o_scr):
    tm = x_ref.shape[0]
    first_tile = pl.program_id(1) == 0
    x = x_ref[...]
    q_scr[...] = _queries(x, g_ref[...], wq_ref[...], gq_ref[...],
                          cos_ref[...], slo_ref[...], shi_ref[...]).astype(BF16)
    kcat = jnp.concatenate([kp_ref[...], kc_ref[...]], axis=0).astype(BF16)
    vtcat = jnp.concatenate([vtp_ref[...], vtc_ref[...]], axis=1).astype(BF16)

    seg = lax.broadcasted_iota(jnp.int32, (1, KV_DIM), 1) // HEAD_DIM
    kj = lax.broadcasted_iota(jnp.int32, (2 * WINDOW, WINDOW), 0)
    qi = lax.broadcasted_iota(jnp.int32, (2 * WINDOW, WINDOW), 1)
    rel = qi + WINDOW - kj
    band = (rel >= 0) & (rel < WINDOW)
    band0 = band & ((kj >= WINDOW) | jnp.logical_not(first_tile))
    bias = jnp.where(band, 0.0, -jnp.inf).astype(F32)
    bias0 = jnp.where(band0, 0.0, -jnp.inf).astype(F32)
    no_rows = jnp.zeros((HEAD_DIM, 2 * WINDOW), BF16)

    for i in range(tm // WINDOW):
        rows = slice(WINDOW * i, WINDOW * (i + 1))
        kw = kcat[WINDOW * i:WINDOW * (i + 2)]
        vtw = vtcat[:, WINDOW * i:WINDOW * (i + 2)]
        q4 = jnp.concatenate(
            [q_scr[rows, KV_DIM * g:KV_DIM * (g + 1)] for g in range(GROUP)], axis=0)
        bias4 = jnp.concatenate([bias0 if i == 0 else bias] * GROUP, axis=1)
        probs, vts = [], []
        for h in range(N_KV_HEADS):
            s = _dot_nt(jnp.where(seg == h, kw, jnp.zeros_like(kw)), q4) + bias4
            sink = jnp.concatenate(
                [jnp.full((1, WINDOW), sinks_ref[N_KV_HEADS * g + h], F32)
                 for g in range(GROUP)], axis=1)
            m = jnp.maximum(jnp.max(s, axis=0, keepdims=True), sink)
            p = jnp.exp(s - m)
            den = jnp.sum(p, axis=0, keepdims=True) + jnp.exp(sink - m)
            probs.append((p * (1.0 / den)).astype(BF16))
            vts.append(jnp.concatenate(
                [vtw[HEAD_DIM * h:HEAD_DIM * (h + 1)] if hh == h else no_rows
                 for hh in range(N_KV_HEADS)], axis=0))
        o_t = _dot(jnp.concatenate(vts, axis=1), jnp.concatenate(probs, axis=0))
        for g in range(GROUP):
            o_scr[rows, KV_DIM * g:KV_DIM * (g + 1)] = (
                o_t[:, WINDOW * g:WINDOW * (g + 1)].T.astype(BF16))

    o_ref[...] = x + _dot(o_scr[...], wo_ref[...])


def _attn_prompt(x, k, vt, g, wq, gq, tables, sinks, wo, tm):
    nb, s, _ = x.shape
    per_tile = tm // WINDOW
    row = pl.BlockSpec((None, tm, D_MODEL), lambda b, t: (b, t, 0))
    cur = pl.BlockSpec((None, tm, KV_DIM), lambda b, t: (b, t, 0))
    prev = pl.BlockSpec((None, WINDOW, KV_DIM),
                        lambda b, t: (b, jnp.maximum(t * per_tile - 1, 0), 0))
    cur_t = pl.BlockSpec((None, KV_DIM, tm), lambda b, t: (b, 0, t))
    prev_t = pl.BlockSpec((None, KV_DIM, WINDOW),
                          lambda b, t: (b, 0, jnp.maximum(t * per_tile - 1, 0)))
    tab = pl.BlockSpec((tm, LANES), lambda b, t: (t, 0))
    return pl.pallas_call(
        _attn_prompt_body, grid=(nb, s // tm),
        in_specs=[pl.BlockSpec(memory_space=pltpu.SMEM), row, cur, prev, cur_t, prev_t,
                  _const_spec((1, D_MODEL)), _const_spec((D_MODEL, D_MODEL)),
                  _const_spec((1, D_MODEL)), tab, tab, tab, _const_spec((D_MODEL, D_MODEL))],
        out_specs=row,
        out_shape=jax.ShapeDtypeStruct(x.shape, F32),
        scratch_shapes=[pltpu.VMEM((tm, D_MODEL), BF16), pltpu.VMEM((tm, D_MODEL), BF16)],
        compiler_params=_params(2), name="attn_prompt")(
            sinks, x, k, k, vt, vt, g, wq, gq, *tables, wo)


def _attn_sample_body(sinks_ref, x_ref, ck_ref, cv_ref, kn_ref, vn_ref, g_ref, wq_ref, gq_ref,
                      cos_ref, slo_ref, shi_ref, wo_ref, o_ref, q_scr, o_scr):
    x = x_ref[...]
    q_scr[...] = _queries(x, g_ref[...], wq_ref[...], gq_ref[...],
                          cos_ref[...], slo_ref[...], shi_ref[...])
    w_buf = ck_ref.shape[1]
    seg_rows = (lax.broadcasted_iota(jnp.int32, (N_KV_HEADS, KV_DIM), 1) // HEAD_DIM
                == lax.broadcasted_iota(jnp.int32, (N_KV_HEADS, KV_DIM), 0))
    in_window = lax.broadcasted_iota(jnp.int32, (1, w_buf), 1) >= 1
    sink = jnp.concatenate([jnp.full((1, 1), sinks_ref[i], F32) for i in range(N_HEADS)], axis=0)

    def one(b, carry):
        kb = ck_ref[b].astype(BF16)
        vb = cv_ref[b].astype(BF16)
        qrow = q_scr[pl.ds(b, 1), :]
        qrows = jnp.concatenate(
            [jnp.where(seg_rows, jnp.broadcast_to(qrow[:, KV_DIM * g:KV_DIM * (g + 1)],
                                                  (N_KV_HEADS, KV_DIM)), 0.0)
             for g in range(GROUP)], axis=0)
        s = jnp.where(in_window, _dot_nt(qrows.astype(BF16), kb), -jnp.inf)
        s_new = jnp.sum(qrows * kn_ref[pl.ds(b, 1), :], axis=-1, keepdims=True)
        m = jnp.maximum(jnp.maximum(jnp.max(s, axis=-1, keepdims=True), s_new), sink)
        p = jnp.exp(s - m)
        p_new = jnp.exp(s_new - m)
        den = jnp.sum(p, axis=-1, keepdims=True) + p_new + jnp.exp(sink - m)
        r = (_dot(p.astype(BF16), vb) + p_new * vn_ref[pl.ds(b, 1), :]) * (1.0 / den)
        o_scr[pl.ds(b, 1), :] = jnp.concatenate(
            [jnp.sum(jnp.where(seg_rows, r[N_KV_HEADS * g:N_KV_HEADS * (g + 1)], 0.0),
                     axis=0, keepdims=True) for g in range(GROUP)], axis=1)
        return carry

    lax.fori_loop(0, x_ref.shape[0], one, 0)
    o_ref[...] = x + _dot(o_scr[...].astype(BF16), wo_ref[...])


def _attn_sample(x, ck, cv, kn, vn, g, wq, gq, tables, sinks, wo, chunk):
    m = x.shape[0]
    w_buf = ck.shape[1]
    row = pl.BlockSpec((chunk, D_MODEL), lambda i: (i, 0))
    tab = pl.BlockSpec((chunk, LANES), lambda i: (i, 0))
    new = pl.BlockSpec((chunk, KV_DIM), lambda i: (i, 0))
    cache = pl.BlockSpec((chunk, w_buf, KV_DIM), lambda i: (i, 0, 0))
    return pl.pallas_call(
        _attn_sample_body, grid=(m // chunk,),
        in_specs=[pl.BlockSpec(memory_space=pltpu.SMEM), row, cache, cache, new, new,
                  _const_spec((1, D_MODEL)), _const_spec((D_MODEL, D_MODEL)),
                  _const_spec((1, D_MODEL)), tab, tab, tab, _const_spec((D_MODEL, D_MODEL))],
        out_specs=row,
        out_shape=jax.ShapeDtypeStruct(x.shape, F32),
        scratch_shapes=[pltpu.VMEM((chunk, D_MODEL), F32), pltpu.VMEM((chunk, D_MODEL), F32)],
        compiler_params=_params(1), name="attn_sample")(
            sinks, x, ck, cv, kn, vn, g, wq, gq, *tables, wo)


def _rope_tables(pos):
    half = ROT_DIM // 2
    inv_freq = ROPE_THETA ** (-jnp.arange(0, ROT_DIM, 2, dtype=F32) / ROT_DIM)
    ang = pos.astype(F32)[:, None] * inv_freq[None, :]
    cos, sin = jnp.cos(ang), jnp.sin(ang)
    n = pos.shape[0]
    rest = HEAD_DIM - ROT_DIM
    cos_h = jnp.concatenate([cos, cos, jnp.ones((n, rest), F32)], axis=1)
    sin_lo = jnp.concatenate([-sin, jnp.zeros((n, half + rest), F32)], axis=1)
    sin_hi = jnp.concatenate([jnp.zeros((n, half), F32), sin, jnp.zeros((n, rest), F32)], axis=1)
    reps = LANES // HEAD_DIM
    return tuple(jnp.tile(t, (1, reps)) for t in (cos_h, sin_lo, sin_hi))


def _head_permutation():
    g, h, d = jnp.meshgrid(jnp.arange(GROUP), jnp.arange(N_KV_HEADS), jnp.arange(HEAD_DIM),
                           indexing="ij")
    return (HEAD_DIM * (GROUP * h + g) + d).reshape(-1)


def kernel(x_prompt, x_sample, state_conv, cache_k, cache_v, g_ffn1, w_ffn1_gate, w_ffn1_up,
           w_ffn1_down, g_mix, g_ffn2, w_ffn2_gate, w_ffn2_up, w_ffn2_down, w_in_a, conv_w,
           w_out_a, g_kv, w_kv, g_knorm, w_q, g_qnorm, sinks, w_o):
    nb, seq, _ = x_prompt.shape
    nd, dec_seq, _ = x_sample.shape
    assert dec_seq == 1
    depth = g_ffn1.shape[0]
    n_a = w_in_a.shape[0]
    past_len = seq
    w_buf = cache_k.shape[1]

    perm = _head_permutation()
    tab_p = _rope_tables(jnp.arange(seq, dtype=jnp.int32))
    tab_s = _rope_tables(jnp.full((nd,), past_len, jnp.int32))
    gk = jnp.tile(g_knorm, N_KV_HEADS)[None]

    xp = x_prompt
    xs = x_sample.reshape(nd, D_MODEL)
    ck = cache_k.reshape(nd, w_buf, KV_DIM)
    cv = cache_v.reshape(nd, w_buf, KV_DIM)
    conv_p, conv_s = [], []
    kp = vp = vtp = kn = vn = new_ck = new_cv = None

    def ffn_both(xp, xs, g, wg, wu, wd):
        g, wg, wu, wd = g[None], wg.astype(BF16), wu.astype(BF16), wd.astype(BF16)
        xp = _ffn(xp.reshape(nb * seq, D_MODEL), g, wg, wu, wd, TM_FFN).reshape(nb, seq, D_MODEL)
        return xp, _ffn(xs, g, wg, wu, wd, nd)

    for i in range(depth):
        if i == n_a:
            wkv = w_kv.astype(BF16)
            kp, vp, vtp = _kv_prompt(xp, g_kv[None], wkv, gk, tab_p, TM_MIX)
            kn, vn, new_ck, new_cv = _kv_sample(xs, g_kv[None], wkv, gk, tab_s, ck, cv,
                                                SAMPLE_CHUNK)
        xp, xs = ffn_both(xp, xs, g_ffn1[i], w_ffn1_gate[i], w_ffn1_up[i], w_ffn1_down[i])
        gm = g_mix[i][None]
        if i < n_a:
            w_in, w_out = w_in_a[i].astype(BF16), w_out_a[i].astype(BF16)
            xp, st = _conv_prompt(xp, jnp.zeros((nb, CONV_W - 1, D_MODEL), F32), gm, w_in,
                                  conv_w[i], w_out, TM_MIX)
            conv_p.append(st)
            xs, st = _conv_sample(xs, state_conv[i].reshape(nd, (CONV_W - 1) * D_MODEL), gm,
                                  w_in, conv_w[i], w_out)
            conv_s.append(st.reshape(nd, CONV_W - 1, D_MODEL))
        else:
            j = i - n_a
            wq = w_q[j][:, perm].astype(BF16)
            wo = w_o[j][perm, :].astype(BF16)
            gq = jnp.tile(g_qnorm[j], N_HEADS)[None]
            sk = sinks[j].reshape(N_KV_HEADS, GROUP).T.reshape(-1)
            xp = _attn_prompt(xp, kp, vtp, gm, wq, gq, tab_p, sk, wo, TM_MIX)
            xs = _attn_sample(xs, ck, cv, kn, vn, gm, wq, gq, tab_s, sk, wo, SAMPLE_CHUNK)
        xp, xs = ffn_both(xp, xs, g_ffn2[i], w_ffn2_gate[i], w_ffn2_up[i], w_ffn2_down[i])

    kv_shape = (w_buf, N_KV_HEADS, HEAD_DIM)
    return (xp, xs.reshape(nd, 1, D_MODEL), jnp.stack(conv_p),
            kp[:, seq - w_buf:].reshape(nb, *kv_shape), vp[:, seq - w_buf:].reshape(nb, *kv_shape),
            jnp.stack(conv_s), new_ck.reshape(nd, *kv_shape), new_cv.reshape(nd, *kv_shape))
```

```python
import functools
import math

import jax
import jax.numpy as jnp
from jax import lax
from jax.experimental import pallas as pl
from jax.experimental.pallas import tpu as pltpu

D_MODEL = 1024
D_FF = 2816
N_HEADS = 16
N_KV_HEADS = 4
GROUP = N_HEADS // N_KV_HEADS
HEAD_DIM = 64
KV_DIM = N_KV_HEADS * HEAD_DIM
ROT_DIM = HEAD_DIM // 4
ROPE_THETA = 500000.0
WINDOW = 128
EPS = 1e-6
CONV_W = 3

LANES = 128
SUBLANES = 8
VMEM_LIMIT = 56 * 1024 * 1024

FFN_CHUNK = 256
TM_FFN = 512
TM_MIX = 512
SAMPLE_CHUNK = 32

F32 = jnp.float32
BF16 = jnp.bfloat16


def _const_spec(shape):
    return pl.BlockSpec(shape, lambda *_: (0,) * len(shape), pipeline_mode=pl.Buffered(1))


def _layer_spec(shape, layer):
    return pl.BlockSpec((None,) + tuple(shape), lambda *_: (layer,) + (0,) * len(shape),
                        pipeline_mode=pl.Buffered(1))


def _params(n_axes):
    return pltpu.CompilerParams(dimension_semantics=("arbitrary",) * n_axes,
                                vmem_limit_bytes=VMEM_LIMIT)


def _dot(a, b):
    return jnp.dot(a, b, preferred_element_type=F32)


def _dot_nt(a, b):
    return lax.dot_general(a, b, (((1,), (1,)), ((), ())), preferred_element_type=F32)


def _rms(x, g):
    ms = jnp.mean(x * x, axis=-1, keepdims=True)
    return x * lax.rsqrt(ms + EPS) * g


def _head_mean_matrix():
    r = lax.broadcasted_iota(jnp.int32, (LANES, LANES), 0) // HEAD_DIM
    c = lax.broadcasted_iota(jnp.int32, (LANES, LANES), 1) // HEAD_DIM
    return jnp.where(r == c, 1.0 / HEAD_DIM, 0.0).astype(BF16)


def _head_norm_rope(x, gain, cos, sin_lo, sin_hi):
    bd = _head_mean_matrix()
    outs = []
    for j in range(x.shape[1] // LANES):
        xs = x[:, LANES * j:LANES * (j + 1)]
        sq = xs * xs
        hi = sq.astype(BF16)
        lo = (sq - hi.astype(F32)).astype(BF16)
        ms = _dot(hi, bd) + _dot(lo, bd)
        xn = xs * lax.rsqrt(ms + EPS) * gain[:, LANES * j:LANES * (j + 1)]
        outs.append(xn * cos + pltpu.roll(xn, ROT_DIM // 2, 1) * sin_hi
                    + pltpu.roll(xn, LANES - ROT_DIM // 2, 1) * sin_lo)
    return jnp.concatenate(outs, axis=1)


def _ffn_body(x_ref, g_ref, wg_ref, wu_ref, wd_ref, o_ref):
    x = x_ref[...]
    h = _rms(x, g_ref[...]).astype(BF16)
    acc = jnp.zeros_like(x)
    for c in range(D_FF // FFN_CHUNK):
        sl = slice(c * FFN_CHUNK, (c + 1) * FFN_CHUNK)
        gate = _dot(h, wg_ref[:, sl])
        up = _dot(h, wu_ref[:, sl])
        a = (gate * jax.nn.sigmoid(gate) * up).astype(BF16)
        acc = acc + _dot(a, wd_ref[sl, :])
    o_ref[...] = x + 0.5 * acc


def _ffn(x, g, wg, wu, wd, layer, tm):
    m = x.shape[0]
    row = pl.BlockSpec((tm, D_MODEL), lambda i: (i, 0))
    return pl.pallas_call(
        _ffn_body, grid=(m // tm,),
        in_specs=[row, _const_spec((1, D_MODEL)), _layer_spec((D_MODEL, D_FF), layer),
                  _layer_spec((D_MODEL, D_FF), layer), _layer_spec((D_FF, D_MODEL), layer)],
        out_specs=row,
        out_shape=jax.ShapeDtypeStruct((m, D_MODEL), F32),
        compiler_params=_params(1), name="ffn")(x, g, wg, wu, wd)


def _conv_prompt_body(x_ref, st_ref, g_ref, win_ref, cw_ref, wout_ref, o_ref, nst_ref, ext_ref):
    tm = x_ref.shape[0]
    lead = SUBLANES - (CONV_W - 1)

    @pl.when(pl.program_id(1) == 0)
    def _():
        ext_ref[lead:SUBLANES, :] = st_ref[...]

    x = x_ref[...]
    h = _rms(x, g_ref[...]).astype(BF16)
    bcu = _dot(h, win_ref[...])
    b = bcu[:, :D_MODEL]
    cu = bcu[:, D_MODEL:2 * D_MODEL] * bcu[:, 2 * D_MODEL:]
    ext_ref[SUBLANES:SUBLANES + tm, :] = cu
    cw = cw_ref[...]
    conv = (cw[0:1] * ext_ref[lead:lead + tm, :] + cw[1:2] * ext_ref[lead + 1:lead + 1 + tm, :]
            + cw[2:3] * cu)
    y = _dot((b * conv).astype(BF16), wout_ref[...])
    o_ref[...] = x + y
    tail = ext_ref[lead + tm:SUBLANES + tm, :]
    ext_ref[lead:SUBLANES, :] = tail
    nst_ref[...] = tail


def _conv_prompt(x, state, g, w_in, cw, w_out, layer, tm):
    nb, s, _ = x.shape
    row = pl.BlockSpec((None, tm, D_MODEL), lambda b, t: (b, t, 0))
    st = pl.BlockSpec((None, CONV_W - 1, D_MODEL), lambda b, t: (b, 0, 0))
    return pl.pallas_call(
        _conv_prompt_body, grid=(nb, s // tm),
        in_specs=[row, st, _const_spec((1, D_MODEL)), _layer_spec((D_MODEL, 3 * D_MODEL), layer),
                  _const_spec((CONV_W, D_MODEL)), _layer_spec((D_MODEL, D_MODEL), layer)],
        out_specs=[row, st],
        out_shape=[jax.ShapeDtypeStruct(x.shape, F32),
                   jax.ShapeDtypeStruct((nb, CONV_W - 1, D_MODEL), F32)],
        scratch_shapes=[pltpu.VMEM((tm + SUBLANES, D_MODEL), F32)],
        compiler_params=_params(2), name="conv_prompt")(x, state, g, w_in, cw, w_out)


def _conv_sample_body(x_ref, st_ref, g_ref, win_ref, cw_ref, wout_ref, o_ref, nst_ref):
    x = x_ref[...]
    h = _rms(x, g_ref[...]).astype(BF16)
    bcu = _dot(h, win_ref[...])
    b = bcu[:, :D_MODEL]
    cu = bcu[:, D_MODEL:2 * D_MODEL] * bcu[:, 2 * D_MODEL:]
    cw = cw_ref[...]
    s1 = st_ref[:, D_MODEL:]
    conv = cw[0:1] * st_ref[:, :D_MODEL] + cw[1:2] * s1 + cw[2:3] * cu
    o_ref[...] = x + _dot((b * conv).astype(BF16), wout_ref[...])
    nst_ref[:, :D_MODEL] = s1
    nst_ref[:, D_MODEL:] = cu


def _conv_sample(x, state, g, w_in, cw, w_out, layer):
    m = x.shape[0]
    return pl.pallas_call(
        _conv_sample_body, grid=(1,),
        in_specs=[_const_spec((m, D_MODEL)), _const_spec((m, 2 * D_MODEL)),
                  _const_spec((1, D_MODEL)), _layer_spec((D_MODEL, 3 * D_MODEL), layer),
                  _const_spec((CONV_W, D_MODEL)), _layer_spec((D_MODEL, D_MODEL), layer)],
        out_specs=[pl.BlockSpec((m, D_MODEL), lambda i: (0, 0)),
                   pl.BlockSpec((m, 2 * D_MODEL), lambda i: (0, 0))],
        out_shape=[jax.ShapeDtypeStruct((m, D_MODEL), F32),
                   jax.ShapeDtypeStruct((m, 2 * D_MODEL), F32)],
        compiler_params=_params(1), name="conv_sample")(x, state, g, w_in, cw, w_out)


def _kv_rows(x, g, w, gk, cos, sin_lo, sin_hi):
    kv = _dot(_rms(x, g).astype(BF16), w)
    k = _head_norm_rope(kv[:, :KV_DIM], gk, cos, sin_lo, sin_hi)
    return k, kv[:, KV_DIM:]


def _kv_prompt_body(x_ref, g_ref, w_ref, gk_ref, cos_ref, slo_ref, shi_ref, k_ref, v_ref, vt_ref):
    k, v = _kv_rows(x_ref[...], g_ref[...], w_ref[...], gk_ref[...],
                    cos_ref[...], slo_ref[...], shi_ref[...])
    k_ref[...] = k
    v_ref[...] = v
    vt_ref[...] = v.T


def _kv_prompt(x, g, w, gk, tables, tm):
    nb, s, _ = x.shape
    row = pl.BlockSpec((None, tm, D_MODEL), lambda b, t: (b, t, 0))
    tab = pl.BlockSpec((tm, LANES), lambda b, t: (t, 0))
    out = pl.BlockSpec((None, tm, KV_DIM), lambda b, t: (b, t, 0))
    out_t = pl.BlockSpec((None, KV_DIM, tm), lambda b, t: (b, 0, t))
    return pl.pallas_call(
        _kv_prompt_body, grid=(nb, s // tm),
        in_specs=[row, _const_spec((1, D_MODEL)), _const_spec((D_MODEL, 2 * KV_DIM)),
                  _const_spec((1, KV_DIM)), tab, tab, tab],
        out_specs=[out, out, out_t],
        out_shape=[jax.ShapeDtypeStruct((nb, s, KV_DIM), F32)] * 2
        + [jax.ShapeDtypeStruct((nb, KV_DIM, s), F32)],
        compiler_params=_params(2), name="kv_prompt")(x, g, w, gk, *tables)


def _kv_sample_body(x_ref, g_ref, w_ref, gk_ref, cos_ref, slo_ref, shi_ref, ck_ref, cv_ref,
                    k_ref, v_ref, nk_ref, nv_ref):
    k, v = _kv_rows(x_ref[...], g_ref[...], w_ref[...], gk_ref[...],
                    cos_ref[...], slo_ref[...], shi_ref[...])
    k_ref[...] = k
    v_ref[...] = v
    w_buf = ck_ref.shape[1]

    def shift(b, carry):
        nk_ref[b, 0:w_buf - 1, :] = ck_ref[b, 1:w_buf, :]
        nv_ref[b, 0:w_buf - 1, :] = cv_ref[b, 1:w_buf, :]
        nk_ref[b, w_buf - 1:w_buf, :] = k_ref[pl.ds(b, 1), :]
        nv_ref[b, w_buf - 1:w_buf, :] = v_ref[pl.ds(b, 1), :]
        return carry

    lax.fori_loop(0, x_ref.shape[0], shift, 0)


def _kv_sample(x, g, w, gk, tables, ck, cv, chunk):
    m = x.shape[0]
    w_buf = ck.shape[1]
    row = pl.BlockSpec((chunk, D_MODEL), lambda i: (i, 0))
    tab = pl.BlockSpec((chunk, LANES), lambda i: (i, 0))
    new = pl.BlockSpec((chunk, KV_DIM), lambda i: (i, 0))
    cache = pl.BlockSpec((chunk, w_buf, KV_DIM), lambda i: (i, 0, 0))
    return pl.pallas_call(
        _kv_sample_body, grid=(m // chunk,),
        in_specs=[row, _const_spec((1, D_MODEL)), _const_spec((D_MODEL, 2 * KV_DIM)),
                  _const_spec((1, KV_DIM)), tab, tab, tab, cache, cache],
        out_specs=[new, new, cache, cache],
        out_shape=[jax.ShapeDtypeStruct((m, KV_DIM), F32)] * 2
        + [jax.ShapeDtypeStruct(ck.shape, F32)] * 2,
        compiler_params=_params(1), name="kv_sample")(x, g, w, gk, *tables, ck, cv)


def _queries(x, g, wq, gq, cos, sin_lo, sin_hi):
    q = _dot(_rms(x, g).astype(BF16), wq)
    return _head_norm_rope(q, gq, cos, sin_lo, sin_hi) * (1.0 / math.sqrt(HEAD_DIM))


def _attn_prompt_body(sinks_ref, x_ref, kc_ref, kp_ref, vtc_ref, vtp_ref, g_ref, wq_ref, gq_ref,
                      cos_ref, slo_ref, shi_ref, wo_ref, o_ref, q_scr, o_scr):
    tm = x_ref.shape[0]
    first_tile = pl.program_id(1) == 0
    x = x_ref[...]
    q_scr[...] = _queries(x, g_ref[...], wq_ref[...], gq_ref[...],
                          cos_ref[...], slo_ref[...], shi_ref[...]).astype(BF16)
    kcat = jnp.concatenate([kp_ref[...], kc_ref[...]], axis=0).astype(BF16)
    vtcat = jnp.concatenate([vtp_ref[...], vtc_ref[...]], axis=1).astype(BF16)

    seg = lax.broadcasted_iota(jnp.int32, (1, KV_DIM), 1) // HEAD_DIM
    kj = lax.broadcasted_iota(jnp.int32, (2 * WINDOW, WINDOW), 0)
    qi = lax.broadcasted_iota(jnp.int32, (2 * WINDOW, WINDOW), 1)
    rel = qi + WINDOW - kj
    band = (rel >= 0) & (rel < WINDOW)
    band0 = band & ((kj >= WINDOW) | jnp.logical_not(first_tile))
    bias = jnp.where(band, 0.0, -jnp.inf).astype(F32)
    bias0 = jnp.where(band0, 0.0, -jnp.inf).astype(F32)
    no_rows = jnp.zeros((HEAD_DIM, 2 * WINDOW), BF16)

    for i in range(tm // WINDOW):
        rows = slice(WINDOW * i, WINDOW * (i + 1))
        kw = kcat[WINDOW * i:WINDOW * (i + 2)]
        vtw = vtcat[:, WINDOW * i:WINDOW * (i + 2)]
        q4 = jnp.concatenate(
            [q_scr[rows, KV_DIM * g:KV_DIM * (g + 1)] for g in range(GROUP)], axis=0)
        bias4 = jnp.concatenate([bias0 if i == 0 else bias] * GROUP, axis=1)
        probs, vts = [], []
        for h in range(N_KV_HEADS):
            s = _dot_nt(jnp.where(seg == h, kw, jnp.zeros_like(kw)), q4) + bias4
            sink = jnp.concatenate(
                [jnp.full((1, WINDOW), sinks_ref[N_KV_HEADS * g + h], F32)
                 for g in range(GROUP)], axis=1)
            m = jnp.maximum(jnp.max(s, axis=0, keepdims=True), sink)
            p = jnp.exp(s - m)
            den = jnp.sum(p, axis=0, keepdims=True) + jnp.exp(sink - m)
            probs.append((p * (1.0 / den)).astype(BF16))
            vts.append(jnp.concatenate(
                [vtw[HEAD_DIM * h:HEAD_DIM * (h + 1)] if hh == h else no_rows
                 for hh in range(N_KV_HEADS)], axis=0))
        o_t = _dot(jnp.concatenate(vts, axis=1), jnp.concatenate(probs, axis=0))
        for g in range(GROUP):
            o_scr[rows, KV_DIM * g:KV_DIM * (g + 1)] = (
                o_t[:, WINDOW * g:WINDOW * (g + 1)].T.astype(BF16))

    o_ref[...] = x + _dot(o_scr[...], wo_ref[...])


def _attn_prompt(x, k, vt, g, wq, gq, tables, sinks, wo, layer, tm):
    nb, s, _ = x.shape
    per_tile = tm // WINDOW
    row = pl.BlockSpec((None, tm, D_MODEL), lambda b, t: (b, t, 0))
    cur = pl.BlockSpec((None, tm, KV_DIM), lambda b, t: (b, t, 0))
    prev = pl.BlockSpec((None, WINDOW, KV_DIM),
                        lambda b, t: (b, jnp.maximum(t * per_tile - 1, 0), 0))
    cur_t = pl.BlockSpec((None, KV_DIM, tm), lambda b, t: (b, 0, t))
    prev_t = pl.BlockSpec((None, KV_DIM, WINDOW),
                          lambda b, t: (b, 0, jnp.maximum(t * per_tile - 1, 0)))
    tab = pl.BlockSpec((tm, LANES), lambda b, t: (t, 0))
    return pl.pallas_call(
        _attn_prompt_body, grid=(nb, s // tm),
        in_specs=[pl.BlockSpec(memory_space=pltpu.SMEM), row, cur, prev, cur_t, prev_t,
                  _const_spec((1, D_MODEL)), _layer_spec((D_MODEL, D_MODEL), layer),
                  _const_spec((1, D_MODEL)), tab, tab, tab, _layer_spec((D_MODEL, D_MODEL), layer)],
        out_specs=row,
        out_shape=jax.ShapeDtypeStruct(x.shape, F32),
        scratch_shapes=[pltpu.VMEM((tm, D_MODEL), BF16), pltpu.VMEM((tm, D_MODEL), BF16)],
        compiler_params=_params(2), name="attn_prompt")(
            sinks, x, k, k, vt, vt, g, wq, gq, *tables, wo)


def _attn_sample_body(sinks_ref, x_ref, ck_ref, cv_ref, kn_ref, vn_ref, g_ref, wq_ref, gq_ref,
                      cos_ref, slo_ref, shi_ref, wo_ref, o_ref, q_scr, o_scr):
    x = x_ref[...]
    q_scr[...] = _queries(x, g_ref[...], wq_ref[...], gq_ref[...],
                          cos_ref[...], slo_ref[...], shi_ref[...])
    w_buf = ck_ref.shape[1]
    seg_rows = (lax.broadcasted_iota(jnp.int32, (N_KV_HEADS, KV_DIM), 1) // HEAD_DIM
                == lax.broadcasted_iota(jnp.int32, (N_KV_HEADS, KV_DIM), 0))
    in_window = lax.broadcasted_iota(jnp.int32, (1, w_buf), 1) >= 1
    sink = jnp.concatenate([jnp.full((1, 1), sinks_ref[i], F32) for i in range(N_HEADS)], axis=0)

    def one(b, carry):
        kb = ck_ref[b].astype(BF16)
        vb = cv_ref[b].astype(BF16)
        qrow = q_scr[pl.ds(b, 1), :]
        qrows = jnp.concatenate(
            [jnp.where(seg_rows, jnp.broadcast_to(qrow[:, KV_DIM * g:KV_DIM * (g + 1)],
                                                  (N_KV_HEADS, KV_DIM)), 0.0)
             for g in range(GROUP)], axis=0)
        s = jnp.where(in_window, _dot_nt(qrows.astype(BF16), kb), -jnp.inf)
        s_new = jnp.sum(qrows * kn_ref[pl.ds(b, 1), :], axis=-1, keepdims=True)
        m = jnp.maximum(jnp.maximum(jnp.max(s, axis=-1, keepdims=True), s_new), sink)
        p = jnp.exp(s - m)
        p_new = jnp.exp(s_new - m)
        den = jnp.sum(p, axis=-1, keepdims=True) + p_new + jnp.exp(sink - m)
        r = (_dot(p.astype(BF16), vb) + p_new * vn_ref[pl.ds(b, 1), :]) * (1.0 / den)
        o_scr[pl.ds(b, 1), :] = jnp.concatenate(
            [jnp.sum(jnp.where(seg_rows, r[N_KV_HEADS * g:N_KV_HEADS * (g + 1)], 0.0),
                     axis=0, keepdims=True) for g in range(GROUP)], axis=1)
        return carry

    lax.fori_loop(0, x_ref.shape[0], one, 0)
    o_ref[...] = x + _dot(o_scr[...].astype(BF16), wo_ref[...])


def _attn_sample(x, ck, cv, kn, vn, g, wq, gq, tables, sinks, wo, layer, chunk):
    m = x.shape[0]
    w_buf = ck.shape[1]
    row = pl.BlockSpec((chunk, D_MODEL), lambda i: (i, 0))
    tab = pl.BlockSpec((chunk, LANES), lambda i: (i, 0))
    new = pl.BlockSpec((chunk, KV_DIM), lambda i: (i, 0))
    cache = pl.BlockSpec((chunk, w_buf, KV_DIM), lambda i: (i, 0, 0))
    return pl.pallas_call(
        _attn_sample_body, grid=(m // chunk,),
        in_specs=[pl.BlockSpec(memory_space=pltpu.SMEM), row, cache, cache, new, new,
                  _const_spec((1, D_MODEL)), _layer_spec((D_MODEL, D_MODEL), layer),
                  _const_spec((1, D_MODEL)), tab, tab, tab, _layer_spec((D_MODEL, D_MODEL), layer)],
        out_specs=row,
        out_shape=jax.ShapeDtypeStruct(x.shape, F32),
        scratch_shapes=[pltpu.VMEM((chunk, D_MODEL), F32), pltpu.VMEM((chunk, D_MODEL), F32)],
        compiler_params=_params(1), name="attn_sample")(
            sinks, x, ck, cv, kn, vn, g, wq, gq, *tables, wo)


def _rope_tables(pos):
    half = ROT_DIM // 2
    inv_freq = ROPE_THETA ** (-jnp.arange(0, ROT_DIM, 2, dtype=F32) / ROT_DIM)
    ang = pos.astype(F32)[:, None] * inv_freq[None, :]
    cos, sin = jnp.cos(ang), jnp.sin(ang)
    n = pos.shape[0]
    rest = HEAD_DIM - ROT_DIM
    cos_h = jnp.concatenate([cos, cos, jnp.ones((n, rest), F32)], axis=1)
    sin_lo = jnp.concatenate([-sin, jnp.zeros((n, half + rest), F32)], axis=1)
    sin_hi = jnp.concatenate([jnp.zeros((n, half), F32), sin, jnp.zeros((n, rest), F32)], axis=1)
    reps = LANES // HEAD_DIM
    return tuple(jnp.tile(t, (1, reps)) for t in (cos_h, sin_lo, sin_hi))


def kernel(x_prompt, x_sample, state_conv, cache_k, cache_v, g_ffn1, w_ffn1_gate, w_ffn1_up,
           w_ffn1_down, g_mix, g_ffn2, w_ffn2_gate, w_ffn2_up, w_ffn2_down, w_in_a, conv_w,
           w_out_a, g_kv, w_kv, g_knorm, w_q, g_qnorm, sinks, w_o):
    nb, seq, _ = x_prompt.shape
    nd, dec_seq, _ = x_sample.shape
    assert dec_seq == 1
    depth = g_ffn1.shape[0]
    n_a = w_in_a.shape[0]
    past_len = seq
    w_buf = cache_k.shape[1]

    n_b = w_q.shape[0]
    bf = lambda w: w.astype(BF16)
    ffn1 = (bf(w_ffn1_gate), bf(w_ffn1_up), bf(w_ffn1_down))
    ffn2 = (bf(w_ffn2_gate), bf(w_ffn2_up), bf(w_ffn2_down))
    w_in, w_out, wkv = bf(w_in_a), bf(w_out_a), bf(w_kv)
    wq = bf(w_q.reshape(n_b, D_MODEL, N_KV_HEADS, GROUP, HEAD_DIM).transpose(0, 1, 3, 2, 4)
            .reshape(n_b, D_MODEL, D_MODEL))
    wo = bf(w_o.reshape(n_b, N_KV_HEADS, GROUP, HEAD_DIM, D_MODEL).transpose(0, 2, 1, 3, 4)
            .reshape(n_b, D_MODEL, D_MODEL))
    sinks_gm = sinks.reshape(n_b, N_KV_HEADS, GROUP).transpose(0, 2, 1).reshape(n_b, N_HEADS)
    tab_p = _rope_tables(jnp.arange(seq, dtype=jnp.int32))
    tab_s = _rope_tables(jnp.full((nd,), past_len, jnp.int32))
    gk = jnp.tile(g_knorm, N_KV_HEADS)[None]

    xp = x_prompt
    xs = x_sample.reshape(nd, D_MODEL)
    ck = cache_k.reshape(nd, w_buf, KV_DIM)
    cv = cache_v.reshape(nd, w_buf, KV_DIM)
    conv_p, conv_s = [], []
    kp = vp = vtp = kn = vn = new_ck = new_cv = None

    def ffn_both(xp, xs, g, weights, layer):
        xp = _ffn(xp.reshape(nb * seq, D_MODEL), g[None], *weights, layer, TM_FFN)
        return xp.reshape(nb, seq, D_MODEL), _ffn(xs, g[None], *weights, layer, nd)

    for i in range(depth):
        if i == n_a:
            kp, vp, vtp = _kv_prompt(xp, g_kv[None], wkv, gk, tab_p, TM_MIX)
            kn, vn, new_ck, new_cv = _kv_sample(xs, g_kv[None], wkv, gk, tab_s, ck, cv,
                                                SAMPLE_CHUNK)
        xp, xs = ffn_both(xp, xs, g_ffn1[i], ffn1, i)
        gm = g_mix[i][None]
        if i < n_a:
            xp, st = _conv_prompt(xp, jnp.zeros((nb, CONV_W - 1, D_MODEL), F32), gm, w_in,
                                  conv_w[i], w_out, i, TM_MIX)
            conv_p.append(st)
            xs, st = _conv_sample(xs, state_conv[i].reshape(nd, (CONV_W - 1) * D_MODEL), gm,
                                  w_in, conv_w[i], w_out, i)
            conv_s.append(st.reshape(nd, CONV_W - 1, D_MODEL))
        else:
            j = i - n_a
            gq = jnp.tile(g_qnorm[j], N_HEADS)[None]
            xp = _attn_prompt(xp, kp, vtp, gm, wq, gq, tab_p, sinks_gm[j], wo, j, TM_MIX)
            xs = _attn_sample(xs, ck, cv, kn, vn, gm, wq, gq, tab_s, sinks_gm[j], wo, j,
                              SAMPLE_CHUNK)
        xp, xs = ffn_both(xp, xs, g_ffn2[i], ffn2, i)

    kv_shape = (w_buf, N_KV_HEADS, HEAD_DIM)
    return (xp, xs.reshape(nd, 1, D_MODEL), jnp.stack(conv_p),
            kp[:, seq - w_buf:].reshape(nb, *kv_shape), vp[:, seq - w_buf:].reshape(nb, *kv_shape),
            jnp.stack(conv_s), new_ck.reshape(nd, *kv_shape), new_cv.reshape(nd, *kv_shape))
```

```python
import functools
import math

import jax
import jax.numpy as jnp
from jax import lax
from jax.experimental import pallas as pl
from jax.experimental.pallas import tpu as pltpu

D_MODEL = 1024
D_FF = 2816
N_HEADS = 16
N_KV_HEADS = 4
GROUP = N_HEADS // N_KV_HEADS
HEAD_DIM = 64
KV_DIM = N_KV_HEADS * HEAD_DIM
ROT_DIM = HEAD_DIM // 4
ROPE_THETA = 500000.0
WINDOW = 128
EPS = 1e-6
CONV_W = 3

LANES = 128
SUBLANES = 8
VMEM_LIMIT = 56 * 1024 * 1024

FFN_CHUNK = 256
TM_FFN = 512
TM_MIX = 512
SAMPLE_CHUNK = 32
SAMPLE_UNROLL = 4

F32 = jnp.float32
BF16 = jnp.bfloat16


def _const_spec(shape):
    return pl.BlockSpec(shape, lambda *_: (0,) * len(shape), pipeline_mode=pl.Buffered(1))


def _layer_spec(shape, layer):
    return pl.BlockSpec((None,) + tuple(shape), lambda *_: (layer,) + (0,) * len(shape),
                        pipeline_mode=pl.Buffered(1))


def _params(n_axes):
    return pltpu.CompilerParams(dimension_semantics=("arbitrary",) * n_axes,
                                vmem_limit_bytes=VMEM_LIMIT)


def _dot(a, b):
    return jnp.dot(a, b, preferred_element_type=F32)


def _dot_nt(a, b):
    return lax.dot_general(a, b, (((1,), (1,)), ((), ())), preferred_element_type=F32)


def _rms(x, g):
    ms = jnp.mean(x * x, axis=-1, keepdims=True)
    return x * lax.rsqrt(ms + EPS) * g


def _head_mean_matrix():
    r = lax.broadcasted_iota(jnp.int32, (LANES, LANES), 0) // HEAD_DIM
    c = lax.broadcasted_iota(jnp.int32, (LANES, LANES), 1) // HEAD_DIM
    return jnp.where(r == c, 1.0 / HEAD_DIM, 0.0).astype(BF16)


def _head_norm_rope(x, gain, cos, sin_lo, sin_hi):
    bd = _head_mean_matrix()
    outs = []
    for j in range(x.shape[1] // LANES):
        xs = x[:, LANES * j:LANES * (j + 1)]
        sq = xs * xs
        hi = sq.astype(BF16)
        lo = (sq - hi.astype(F32)).astype(BF16)
        ms = _dot(hi, bd) + _dot(lo, bd)
        xn = xs * lax.rsqrt(ms + EPS) * gain[:, LANES * j:LANES * (j + 1)]
        outs.append(xn * cos + pltpu.roll(xn, ROT_DIM // 2, 1) * sin_hi
                    + pltpu.roll(xn, LANES - ROT_DIM // 2, 1) * sin_lo)
    return jnp.concatenate(outs, axis=1)


def _ffn_body(x_ref, g_ref, wg_ref, wu_ref, wd_ref, o_ref):
    x = x_ref[...]
    h = _rms(x, g_ref[...])
    acc = jnp.zeros_like(x)
    for c in range(D_FF // FFN_CHUNK):
        sl = slice(c * FFN_CHUNK, (c + 1) * FFN_CHUNK)
        gate = _dot(h, wg_ref[:, sl])
        up = _dot(h, wu_ref[:, sl])
        a = gate * jax.nn.sigmoid(gate) * up
        acc = acc + _dot(a, wd_ref[sl, :])
    o_ref[...] = x + 0.5 * acc


def _ffn(x, g, wg, wu, wd, layer, tm):
    m = x.shape[0]
    row = pl.BlockSpec((tm, D_MODEL), lambda i: (i, 0))
    return pl.pallas_call(
        _ffn_body, grid=(m // tm,),
        in_specs=[row, _const_spec((1, D_MODEL)), _layer_spec((D_MODEL, D_FF), layer),
                  _layer_spec((D_MODEL, D_FF), layer), _layer_spec((D_FF, D_MODEL), layer)],
        out_specs=row,
        out_shape=jax.ShapeDtypeStruct((m, D_MODEL), F32),
        compiler_params=_params(1), name="ffn")(x, g, wg, wu, wd)


def _conv_prompt_body(x_ref, st_ref, g_ref, win_ref, cw_ref, wout_ref, o_ref, nst_ref, ext_ref):
    tm = x_ref.shape[0]
    lead = SUBLANES - (CONV_W - 1)

    @pl.when(pl.program_id(1) == 0)
    def _():
        ext_ref[lead:SUBLANES, :] = st_ref[...]

    x = x_ref[...]
    bcu = _dot(_rms(x, g_ref[...]), win_ref[...])
    b = bcu[:, :D_MODEL]
    cu = bcu[:, D_MODEL:2 * D_MODEL] * bcu[:, 2 * D_MODEL:]
    ext_ref[SUBLANES:SUBLANES + tm, :] = cu
    cw = cw_ref[...]
    conv = (cw[0:1] * ext_ref[lead:lead + tm, :] + cw[1:2] * ext_ref[lead + 1:lead + 1 + tm, :]
            + cw[2:3] * cu)
    y = _dot(b * conv, wout_ref[...])
    o_ref[...] = x + y
    tail = ext_ref[lead + tm:SUBLANES + tm, :]
    ext_ref[lead:SUBLANES, :] = tail
    nst_ref[...] = tail


def _conv_prompt(x, state, g, w_in, cw, w_out, layer, tm):
    nb, s, _ = x.shape
    row = pl.BlockSpec((None, tm, D_MODEL), lambda b, t: (b, t, 0))
    st = pl.BlockSpec((None, CONV_W - 1, D_MODEL), lambda b, t: (b, 0, 0))
    return pl.pallas_call(
        _conv_prompt_body, grid=(nb, s // tm),
        in_specs=[row, st, _const_spec((1, D_MODEL)), _layer_spec((D_MODEL, 3 * D_MODEL), layer),
                  _const_spec((CONV_W, D_MODEL)), _layer_spec((D_MODEL, D_MODEL), layer)],
        out_specs=[row, st],
        out_shape=[jax.ShapeDtypeStruct(x.shape, F32),
                   jax.ShapeDtypeStruct((nb, CONV_W - 1, D_MODEL), F32)],
        scratch_shapes=[pltpu.VMEM((tm + SUBLANES, D_MODEL), F32)],
        compiler_params=_params(2), name="conv_prompt")(x, state, g, w_in, cw, w_out)


def _conv_sample_body(x_ref, st_ref, g_ref, win_ref, cw_ref, wout_ref, o_ref, nst_ref):
    x = x_ref[...]
    bcu = _dot(_rms(x, g_ref[...]), win_ref[...])
    b = bcu[:, :D_MODEL]
    cu = bcu[:, D_MODEL:2 * D_MODEL] * bcu[:, 2 * D_MODEL:]
    cw = cw_ref[...]
    s1 = st_ref[:, D_MODEL:]
    conv = cw[0:1] * st_ref[:, :D_MODEL] + cw[1:2] * s1 + cw[2:3] * cu
    o_ref[...] = x + _dot(b * conv, wout_ref[...])
    nst_ref[:, :D_MODEL] = s1
    nst_ref[:, D_MODEL:] = cu


def _conv_sample(x, state, g, w_in, cw, w_out, layer):
    m = x.shape[0]
    return pl.pallas_call(
        _conv_sample_body, grid=(1,),
        in_specs=[_const_spec((m, D_MODEL)), _const_spec((m, 2 * D_MODEL)),
                  _const_spec((1, D_MODEL)), _layer_spec((D_MODEL, 3 * D_MODEL), layer),
                  _const_spec((CONV_W, D_MODEL)), _layer_spec((D_MODEL, D_MODEL), layer)],
        out_specs=[pl.BlockSpec((m, D_MODEL), lambda i: (0, 0)),
                   pl.BlockSpec((m, 2 * D_MODEL), lambda i: (0, 0))],
        out_shape=[jax.ShapeDtypeStruct((m, D_MODEL), F32),
                   jax.ShapeDtypeStruct((m, 2 * D_MODEL), F32)],
        compiler_params=_params(1), name="conv_sample")(x, state, g, w_in, cw, w_out)


def _kv_rows(x, g, w, gk, cos, sin_lo, sin_hi):
    kv = _dot(_rms(x, g), w)
    k = _head_norm_rope(kv[:, :KV_DIM], gk, cos, sin_lo, sin_hi)
    return k, kv[:, KV_DIM:]


def _kv_prompt_body(x_ref, g_ref, w_ref, gk_ref, cos_ref, slo_ref, shi_ref, k_ref, v_ref, vt_ref):
    k, v = _kv_rows(x_ref[...], g_ref[...], w_ref[...], gk_ref[...],
                    cos_ref[...], slo_ref[...], shi_ref[...])
    k_ref[...] = k
    v_ref[...] = v
    vt_ref[...] = v.T


def _kv_prompt(x, g, w, gk, tables, tm):
    nb, s, _ = x.shape
    row = pl.BlockSpec((None, tm, D_MODEL), lambda b, t: (b, t, 0))
    tab = pl.BlockSpec((tm, LANES), lambda b, t: (t, 0))
    out = pl.BlockSpec((None, tm, KV_DIM), lambda b, t: (b, t, 0))
    out_t = pl.BlockSpec((None, KV_DIM, tm), lambda b, t: (b, 0, t))
    return pl.pallas_call(
        _kv_prompt_body, grid=(nb, s // tm),
        in_specs=[row, _const_spec((1, D_MODEL)), _const_spec((D_MODEL, 2 * KV_DIM)),
                  _const_spec((1, KV_DIM)), tab, tab, tab],
        out_specs=[out, out, out_t],
        out_shape=[jax.ShapeDtypeStruct((nb, s, KV_DIM), F32)] * 2
        + [jax.ShapeDtypeStruct((nb, KV_DIM, s), F32)],
        compiler_params=_params(2), name="kv_prompt")(x, g, w, gk, *tables)


def _kv_sample_body(x_ref, g_ref, w_ref, gk_ref, cos_ref, slo_ref, shi_ref, ck_ref, cv_ref,
                    k_ref, v_ref, nk_ref, nv_ref):
    k, v = _kv_rows(x_ref[...], g_ref[...], w_ref[...], gk_ref[...],
                    cos_ref[...], slo_ref[...], shi_ref[...])
    k_ref[...] = k
    v_ref[...] = v
    w_buf = ck_ref.shape[1]

    def shift(b, carry):
        nk_ref[b, 0:w_buf - 1, :] = ck_ref[b, 1:w_buf, :]
        nv_ref[b, 0:w_buf - 1, :] = cv_ref[b, 1:w_buf, :]
        nk_ref[b, w_buf - 1:w_buf, :] = k_ref[pl.ds(b, 1), :]
        nv_ref[b, w_buf - 1:w_buf, :] = v_ref[pl.ds(b, 1), :]
        return carry

    lax.fori_loop(0, x_ref.shape[0], shift, 0)


def _kv_sample(x, g, w, gk, tables, ck, cv, chunk):
    m = x.shape[0]
    w_buf = ck.shape[1]
    row = pl.BlockSpec((chunk, D_MODEL), lambda i: (i, 0))
    tab = pl.BlockSpec((chunk, LANES), lambda i: (i, 0))
    new = pl.BlockSpec((chunk, KV_DIM), lambda i: (i, 0))
    cache = pl.BlockSpec((chunk, w_buf, KV_DIM), lambda i: (i, 0, 0))
    return pl.pallas_call(
        _kv_sample_body, grid=(m // chunk,),
        in_specs=[row, _const_spec((1, D_MODEL)), _const_spec((D_MODEL, 2 * KV_DIM)),
                  _const_spec((1, KV_DIM)), tab, tab, tab, cache, cache],
        out_specs=[new, new, cache, cache],
        out_shape=[jax.ShapeDtypeStruct((m, KV_DIM), F32)] * 2
        + [jax.ShapeDtypeStruct(ck.shape, F32)] * 2,
        compiler_params=_params(1), name="kv_sample")(x, g, w, gk, *tables, ck, cv)


def _queries(x, g, wq, gq, cos, sin_lo, sin_hi):
    q = _dot(_rms(x, g), wq)
    return _head_norm_rope(q, gq, cos, sin_lo, sin_hi) * (1.0 / math.sqrt(HEAD_DIM))


def _attn_prompt_body(sinks_ref, x_ref, kc_ref, kp_ref, vtc_ref, vtp_ref, g_ref, wq_ref, gq_ref,
                      cos_ref, slo_ref, shi_ref, wo_ref, o_ref, q_scr, o_scr):
    tm = x_ref.shape[0]
    first_tile = pl.program_id(1) == 0
    x = x_ref[...]
    q_scr[...] = _queries(x, g_ref[...], wq_ref[...], gq_ref[...],
                          cos_ref[...], slo_ref[...], shi_ref[...]).astype(BF16)
    kcat = jnp.concatenate([kp_ref[...], kc_ref[...]], axis=0).astype(BF16)
    vtcat = jnp.concatenate([vtp_ref[...], vtc_ref[...]], axis=1).astype(BF16)

    seg = lax.broadcasted_iota(jnp.int32, (1, KV_DIM), 1) // HEAD_DIM
    kj = lax.broadcasted_iota(jnp.int32, (2 * WINDOW, WINDOW), 0)
    qi = lax.broadcasted_iota(jnp.int32, (2 * WINDOW, WINDOW), 1)
    rel = qi + WINDOW - kj
    band = (rel >= 0) & (rel < WINDOW)
    band0 = band & ((kj >= WINDOW) | jnp.logical_not(first_tile))
    bias = jnp.where(band, 0.0, -jnp.inf).astype(F32)
    bias0 = jnp.where(band0, 0.0, -jnp.inf).astype(F32)
    no_rows = jnp.zeros((HEAD_DIM, 2 * WINDOW), BF16)

    for i in range(tm // WINDOW):
        rows = slice(WINDOW * i, WINDOW * (i + 1))
        kw = kcat[WINDOW * i:WINDOW * (i + 2)]
        vtw = vtcat[:, WINDOW * i:WINDOW * (i + 2)]
        q4 = jnp.concatenate(
            [q_scr[rows, KV_DIM * g:KV_DIM * (g + 1)] for g in range(GROUP)], axis=0)
        bias4 = jnp.concatenate([bias0 if i == 0 else bias] * GROUP, axis=1)
        probs, vts = [], []
        for h in range(N_KV_HEADS):
            s = _dot_nt(jnp.where(seg == h, kw, jnp.zeros_like(kw)), q4) + bias4
            sink = jnp.concatenate(
                [jnp.full((1, WINDOW), sinks_ref[N_KV_HEADS * g + h], F32)
                 for g in range(GROUP)], axis=1)
            m = jnp.maximum(jnp.max(s, axis=0, keepdims=True), sink)
            p = jnp.exp(s - m)
            den = jnp.sum(p, axis=0, keepdims=True) + jnp.exp(sink - m)
            probs.append((p * (1.0 / den)).astype(BF16))
            vts.append(jnp.concatenate(
                [vtw[HEAD_DIM * h:HEAD_DIM * (h + 1)] if hh == h else no_rows
                 for hh in range(N_KV_HEADS)], axis=0))
        o_t = _dot(jnp.concatenate(vts, axis=1), jnp.concatenate(probs, axis=0))
        for g in range(GROUP):
            o_scr[rows, KV_DIM * g:KV_DIM * (g + 1)] = (
                o_t[:, WINDOW * g:WINDOW * (g + 1)].T)

    o_ref[...] = x + _dot(o_scr[...], wo_ref[...])


def _attn_prompt(x, k, vt, g, wq, gq, tables, sinks, wo, layer, tm):
    nb, s, _ = x.shape
    per_tile = tm // WINDOW
    row = pl.BlockSpec((None, tm, D_MODEL), lambda b, t: (b, t, 0))
    cur = pl.BlockSpec((None, tm, KV_DIM), lambda b, t: (b, t, 0))
    prev = pl.BlockSpec((None, WINDOW, KV_DIM),
                        lambda b, t: (b, jnp.maximum(t * per_tile - 1, 0), 0))
    cur_t = pl.BlockSpec((None, KV_DIM, tm), lambda b, t: (b, 0, t))
    prev_t = pl.BlockSpec((None, KV_DIM, WINDOW),
                          lambda b, t: (b, 0, jnp.maximum(t * per_tile - 1, 0)))
    tab = pl.BlockSpec((tm, LANES), lambda b, t: (t, 0))
    return pl.pallas_call(
        _attn_prompt_body, grid=(nb, s // tm),
        in_specs=[pl.BlockSpec(memory_space=pltpu.SMEM), row, cur, prev, cur_t, prev_t,
                  _const_spec((1, D_MODEL)), _layer_spec((D_MODEL, D_MODEL), layer),
                  _const_spec((1, D_MODEL)), tab, tab, tab, _layer_spec((D_MODEL, D_MODEL), layer)],
        out_specs=row,
        out_shape=jax.ShapeDtypeStruct(x.shape, F32),
        scratch_shapes=[pltpu.VMEM((tm, D_MODEL), BF16), pltpu.VMEM((tm, D_MODEL), F32)],
        compiler_params=_params(2), name="attn_prompt")(
            sinks, x, k, k, vt, vt, g, wq, gq, *tables, wo)


def _attn_sample_body(sinks_ref, x_ref, ck_ref, cv_ref, kn_ref, vn_ref, g_ref, wq_ref, gq_ref,
                      cos_ref, slo_ref, shi_ref, wo_ref, o_ref, q_scr, o_scr):
    x = x_ref[...]
    q_scr[...] = _queries(x, g_ref[...], wq_ref[...], gq_ref[...],
                          cos_ref[...], slo_ref[...], shi_ref[...])
    w_buf = ck_ref.shape[1]
    seg_rows = (lax.broadcasted_iota(jnp.int32, (N_KV_HEADS, KV_DIM), 1) // HEAD_DIM
                == lax.broadcasted_iota(jnp.int32, (N_KV_HEADS, KV_DIM), 0))
    in_window = lax.broadcasted_iota(jnp.int32, (1, w_buf), 1) >= 1
    sink = jnp.concatenate([jnp.full((1, 1), sinks_ref[i], F32) for i in range(N_HEADS)], axis=0)

    def one(b, carry):
        kb = ck_ref[b].astype(BF16)
        vb = cv_ref[b].astype(BF16)
        qrow = q_scr[pl.ds(b, 1), :]
        qrows = jnp.concatenate(
            [jnp.where(seg_rows, jnp.broadcast_to(qrow[:, KV_DIM * g:KV_DIM * (g + 1)],
                                                  (N_KV_HEADS, KV_DIM)), 0.0)
             for g in range(GROUP)], axis=0)
        s = jnp.where(in_window, _dot_nt(qrows.astype(BF16), kb), -jnp.inf)
        s_new = jnp.sum(qrows * kn_ref[pl.ds(b, 1), :], axis=-1, keepdims=True)
        m = jnp.maximum(jnp.maximum(jnp.max(s, axis=-1, keepdims=True), s_new), sink)
        p = jnp.exp(s - m)
        p_new = jnp.exp(s_new - m)
        den = jnp.sum(p, axis=-1, keepdims=True) + p_new + jnp.exp(sink - m)
        r = (_dot(p.astype(BF16), vb) + p_new * vn_ref[pl.ds(b, 1), :]) * (1.0 / den)
        o_scr[pl.ds(b, 1), :] = jnp.concatenate(
            [jnp.sum(jnp.where(seg_rows, r[N_KV_HEADS * g:N_KV_HEADS * (g + 1)], 0.0),
                     axis=0, keepdims=True) for g in range(GROUP)], axis=1)
        return carry

    lax.fori_loop(0, x_ref.shape[0], one, 0, unroll=SAMPLE_UNROLL)
    o_ref[...] = x + _dot(o_scr[...], wo_ref[...])


def _attn_sample(x, ck, cv, kn, vn, g, wq, gq, tables, sinks, wo, layer, chunk):
    m = x.shape[0]
    w_buf = ck.shape[1]
    row = pl.BlockSpec((chunk, D_MODEL), lambda i: (i, 0))
    tab = pl.BlockSpec((chunk, LANES), lambda i: (i, 0))
    new = pl.BlockSpec((chunk, KV_DIM), lambda i: (i, 0))
    cache = pl.BlockSpec((chunk, w_buf, KV_DIM), lambda i: (i, 0, 0))
    return pl.pallas_call(
        _attn_sample_body, grid=(m // chunk,),
        in_specs=[pl.BlockSpec(memory_space=pltpu.SMEM), row, cache, cache, new, new,
                  _const_spec((1, D_MODEL)), _layer_spec((D_MODEL, D_MODEL), layer),
                  _const_spec((1, D_MODEL)), tab, tab, tab, _layer_spec((D_MODEL, D_MODEL), layer)],
        out_specs=row,
        out_shape=jax.ShapeDtypeStruct(x.shape, F32),
        scratch_shapes=[pltpu.VMEM((chunk, D_MODEL), F32), pltpu.VMEM((chunk, D_MODEL), F32)],
        compiler_params=_params(1), name="attn_sample")(
            sinks, x, ck, cv, kn, vn, g, wq, gq, *tables, wo)


def _rope_tables(pos):
    half = ROT_DIM // 2
    inv_freq = ROPE_THETA ** (-jnp.arange(0, ROT_DIM, 2, dtype=F32) / ROT_DIM)
    ang = pos.astype(F32)[:, None] * inv_freq[None, :]
    cos, sin = jnp.cos(ang), jnp.sin(ang)
    n = pos.shape[0]
    rest = HEAD_DIM - ROT_DIM
    cos_h = jnp.concatenate([cos, cos, jnp.ones((n, rest), F32)], axis=1)
    sin_lo = jnp.concatenate([-sin, jnp.zeros((n, half + rest), F32)], axis=1)
    sin_hi = jnp.concatenate([jnp.zeros((n, half), F32), sin, jnp.zeros((n, rest), F32)], axis=1)
    reps = LANES // HEAD_DIM
    return tuple(jnp.tile(t, (1, reps)) for t in (cos_h, sin_lo, sin_hi))


def kernel(x_prompt, x_sample, state_conv, cache_k, cache_v, g_ffn1, w_ffn1_gate, w_ffn1_up,
           w_ffn1_down, g_mix, g_ffn2, w_ffn2_gate, w_ffn2_up, w_ffn2_down, w_in_a, conv_w,
           w_out_a, g_kv, w_kv, g_knorm, w_q, g_qnorm, sinks, w_o):
    nb, seq, _ = x_prompt.shape
    nd, dec_seq, _ = x_sample.shape
    assert dec_seq == 1
    depth = g_ffn1.shape[0]
    n_a = w_in_a.shape[0]
    past_len = seq
    w_buf = cache_k.shape[1]

    n_b = w_q.shape[0]
    ffn1 = (w_ffn1_gate, w_ffn1_up, w_ffn1_down)
    ffn2 = (w_ffn2_gate, w_ffn2_up, w_ffn2_down)
    w_in, w_out, wkv = w_in_a, w_out_a, w_kv
    wq = (w_q.reshape(n_b, D_MODEL, N_KV_HEADS, GROUP, HEAD_DIM).transpose(0, 1, 3, 2, 4)
          .reshape(n_b, D_MODEL, D_MODEL))
    wo = (w_o.reshape(n_b, N_KV_HEADS, GROUP, HEAD_DIM, D_MODEL).transpose(0, 2, 1, 3, 4)
          .reshape(n_b, D_MODEL, D_MODEL))
    sinks_gm = sinks.reshape(n_b, N_KV_HEADS, GROUP).transpose(0, 2, 1).reshape(n_b, N_HEADS)
    tab_p = _rope_tables(jnp.arange(seq, dtype=jnp.int32))
    tab_s = _rope_tables(jnp.full((nd,), past_len, jnp.int32))
    gk = jnp.tile(g_knorm, N_KV_HEADS)[None]

    xp = x_prompt
    xs = x_sample.reshape(nd, D_MODEL)
    ck = cache_k.reshape(nd, w_buf, KV_DIM)
    cv = cache_v.reshape(nd, w_buf, KV_DIM)
    conv_p, conv_s = [], []
    kp = vp = vtp = kn = vn = new_ck = new_cv = None

    def ffn_both(xp, xs, g, weights, layer):
        xp = _ffn(xp.reshape(nb * seq, D_MODEL), g[None], *weights, layer, TM_FFN)
        return xp.reshape(nb, seq, D_MODEL), _ffn(xs, g[None], *weights, layer, nd)

    for i in range(depth):
        if i == n_a:
            kp, vp, vtp = _kv_prompt(xp, g_kv[None], wkv, gk, tab_p, TM_MIX)
            kn, vn, new_ck, new_cv = _kv_sample(xs, g_kv[None], wkv, gk, tab_s, ck, cv,
                                                SAMPLE_CHUNK)
        xp, xs = ffn_both(xp, xs, g_ffn1[i], ffn1, i)
        gm = g_mix[i][None]
        if i < n_a:
            xp, st = _conv_prompt(xp, jnp.zeros((nb, CONV_W - 1, D_MODEL), F32), gm, w_in,
                                  conv_w[i], w_out, i, TM_MIX)
            conv_p.append(st)
            xs, st = _conv_sample(xs, state_conv[i].reshape(nd, (CONV_W - 1) * D_MODEL), gm,
                                  w_in, conv_w[i], w_out, i)
            conv_s.append(st.reshape(nd, CONV_W - 1, D_MODEL))
        else:
            j = i - n_a
            gq = jnp.tile(g_qnorm[j], N_HEADS)[None]
            xp = _attn_prompt(xp, kp, vtp, gm, wq, gq, tab_p, sinks_gm[j], wo, j, TM_MIX)
            xs = _attn_sample(xs, ck, cv, kn, vn, gm, wq, gq, tab_s, sinks_gm[j], wo, j,
                              SAMPLE_CHUNK)
        xp, xs = ffn_both(xp, xs, g_ffn2[i], ffn2, i)

    kv_shape = (w_buf, N_KV_HEADS, HEAD_DIM)
    return (xp, xs.reshape(nd, 1, D_MODEL), jnp.stack(conv_p),
            kp[:, seq - w_buf:].reshape(nb, *kv_shape), vp[:, seq - w_buf:].reshape(nb, *kv_shape),
            jnp.stack(conv_s), new_ck.reshape(nd, *kv_shape), new_cv.reshape(nd, *kv_shape))
```

```python
import functools
import math

import jax
import jax.numpy as jnp
from jax import lax
from jax.experimental import pallas as pl
from jax.experimental.pallas import tpu as pltpu

D_MODEL = 1024
D_FF = 2816
N_HEADS = 16
N_KV_HEADS = 4
GROUP = N_HEADS // N_KV_HEADS
HEAD_DIM = 64
KV_DIM = N_KV_HEADS * HEAD_DIM
ROT_DIM = HEAD_DIM // 4
ROPE_THETA = 500000.0
WINDOW = 128
EPS = 1e-6
CONV_W = 3

LANES = 128
SUBLANES = 8
VMEM_LIMIT = 56 * 1024 * 1024

FFN_CHUNK = 256
TM_FFN = 512
TM_MIX = 512
SAMPLE_CHUNK = 32
SAMPLE_UNROLL = 4

F32 = jnp.float32
BF16 = jnp.bfloat16


def _const_spec(shape):
    return pl.BlockSpec(shape, lambda *_: (0,) * len(shape), pipeline_mode=pl.Buffered(1))


def _layer_spec(shape, layer):
    return pl.BlockSpec((None,) + tuple(shape), lambda *_: (layer,) + (0,) * len(shape),
                        pipeline_mode=pl.Buffered(1))


def _params(n_axes):
    return pltpu.CompilerParams(dimension_semantics=("arbitrary",) * n_axes,
                                vmem_limit_bytes=VMEM_LIMIT)


def _dot(a, b):
    return jnp.dot(a, b, preferred_element_type=F32)


def _dot_nt(a, b):
    return lax.dot_general(a, b, (((1,), (1,)), ((), ())), preferred_element_type=F32)


def _rms(x, g):
    ms = jnp.mean(x * x, axis=-1, keepdims=True)
    return x * lax.rsqrt(ms + EPS) * g


def _head_mean_matrix():
    r = lax.broadcasted_iota(jnp.int32, (LANES, LANES), 0) // HEAD_DIM
    c = lax.broadcasted_iota(jnp.int32, (LANES, LANES), 1) // HEAD_DIM
    return jnp.where(r == c, 1.0 / HEAD_DIM, 0.0).astype(BF16)


def _head_norm_rope(x, gain, cos, sin_lo, sin_hi):
    bd = _head_mean_matrix()
    outs = []
    for j in range(x.shape[1] // LANES):
        xs = x[:, LANES * j:LANES * (j + 1)]
        sq = xs * xs
        hi = sq.astype(BF16)
        lo = (sq - hi.astype(F32)).astype(BF16)
        ms = _dot(hi, bd) + _dot(lo, bd)
        xn = xs * lax.rsqrt(ms + EPS) * gain[:, LANES * j:LANES * (j + 1)]
        outs.append(xn * cos + pltpu.roll(xn, ROT_DIM // 2, 1) * sin_hi
                    + pltpu.roll(xn, LANES - ROT_DIM // 2, 1) * sin_lo)
    return jnp.concatenate(outs, axis=1)


def _ffn_rows(x, g, wg_ref, wu_ref, wd_ref):
    h = _rms(x, g)
    acc = jnp.zeros_like(x)
    for c in range(D_FF // FFN_CHUNK):
        sl = slice(c * FFN_CHUNK, (c + 1) * FFN_CHUNK)
        gate = _dot(h, wg_ref[:, sl])
        up = _dot(h, wu_ref[:, sl])
        a = gate * jax.nn.sigmoid(gate) * up
        acc = acc + _dot(a, wd_ref[sl, :])
    return x + 0.5 * acc


def _ffn_body(xp_ref, xs_ref, g_ref, wg_ref, wu_ref, wd_ref, op_ref, os_ref):
    step = pl.program_id(0)
    last = pl.num_programs(0) - 1

    @pl.when(step < last)
    def _():
        op_ref[...] = _ffn_rows(xp_ref[...], g_ref[...], wg_ref, wu_ref, wd_ref)

    @pl.when(step == last)
    def _():
        os_ref[...] = _ffn_rows(xs_ref[...], g_ref[...], wg_ref, wu_ref, wd_ref)


def _ffn(xp, xs, g, wg, wu, wd, layer, tm):
    m, ms = xp.shape[0], xs.shape[0]
    n_p = m // tm
    row_p = pl.BlockSpec((tm, D_MODEL), lambda i: (jnp.minimum(i, n_p - 1), 0))
    return pl.pallas_call(
        _ffn_body, grid=(n_p + 1,),
        in_specs=[row_p, _const_spec((ms, D_MODEL)), _const_spec((1, D_MODEL)),
                  _layer_spec((D_MODEL, D_FF), layer), _layer_spec((D_MODEL, D_FF), layer),
                  _layer_spec((D_FF, D_MODEL), layer)],
        out_specs=[row_p, pl.BlockSpec((ms, D_MODEL), lambda i: (0, 0))],
        out_shape=[jax.ShapeDtypeStruct((m, D_MODEL), F32),
                   jax.ShapeDtypeStruct((ms, D_MODEL), F32)],
        compiler_params=_params(1), name="ffn")(xp, xs, g, wg, wu, wd)


def _conv_prompt_body(x_ref, st_ref, g_ref, win_ref, cw_ref, wout_ref, o_ref, nst_ref, ext_ref):
    tm = x_ref.shape[0]
    lead = SUBLANES - (CONV_W - 1)

    @pl.when(pl.program_id(1) == 0)
    def _():
        ext_ref[lead:SUBLANES, :] = st_ref[...]

    x = x_ref[...]
    bcu = _dot(_rms(x, g_ref[...]), win_ref[...])
    b = bcu[:, :D_MODEL]
    cu = bcu[:, D_MODEL:2 * D_MODEL] * bcu[:, 2 * D_MODEL:]
    ext_ref[SUBLANES:SUBLANES + tm, :] = cu
    cw = cw_ref[...]
    conv = (cw[0:1] * ext_ref[lead:lead + tm, :] + cw[1:2] * ext_ref[lead + 1:lead + 1 + tm, :]
            + cw[2:3] * cu)
    y = _dot(b * conv, wout_ref[...])
    o_ref[...] = x + y
    tail = ext_ref[lead + tm:SUBLANES + tm, :]
    ext_ref[lead:SUBLANES, :] = tail
    nst_ref[...] = tail


def _conv_prompt(x, state, g, w_in, cw, w_out, layer, tm):
    nb, s, _ = x.shape
    row = pl.BlockSpec((None, tm, D_MODEL), lambda b, t: (b, t, 0))
    st = pl.BlockSpec((None, CONV_W - 1, D_MODEL), lambda b, t: (b, 0, 0))
    return pl.pallas_call(
        _conv_prompt_body, grid=(nb, s // tm),
        in_specs=[row, st, _const_spec((1, D_MODEL)), _layer_spec((D_MODEL, 3 * D_MODEL), layer),
                  _const_spec((CONV_W, D_MODEL)), _layer_spec((D_MODEL, D_MODEL), layer)],
        out_specs=[row, st],
        out_shape=[jax.ShapeDtypeStruct(x.shape, F32),
                   jax.ShapeDtypeStruct((nb, CONV_W - 1, D_MODEL), F32)],
        scratch_shapes=[pltpu.VMEM((tm + SUBLANES, D_MODEL), F32)],
        compiler_params=_params(2), name="conv_prompt")(x, state, g, w_in, cw, w_out)


def _conv_sample_body(x_ref, st_ref, g_ref, win_ref, cw_ref, wout_ref, o_ref, nst_ref):
    x = x_ref[...]
    bcu = _dot(_rms(x, g_ref[...]), win_ref[...])
    b = bcu[:, :D_MODEL]
    cu = bcu[:, D_MODEL:2 * D_MODEL] * bcu[:, 2 * D_MODEL:]
    cw = cw_ref[...]
    s1 = st_ref[:, D_MODEL:]
    conv = cw[0:1] * st_ref[:, :D_MODEL] + cw[1:2] * s1 + cw[2:3] * cu
    o_ref[...] = x + _dot(b * conv, wout_ref[...])
    nst_ref[:, :D_MODEL] = s1
    nst_ref[:, D_MODEL:] = cu


def _conv_sample(x, state, g, w_in, cw, w_out, layer):
    m = x.shape[0]
    return pl.pallas_call(
        _conv_sample_body, grid=(1,),
        in_specs=[_const_spec((m, D_MODEL)), _const_spec((m, 2 * D_MODEL)),
                  _const_spec((1, D_MODEL)), _layer_spec((D_MODEL, 3 * D_MODEL), layer),
                  _const_spec((CONV_W, D_MODEL)), _layer_spec((D_MODEL, D_MODEL), layer)],
        out_specs=[pl.BlockSpec((m, D_MODEL), lambda i: (0, 0)),
                   pl.BlockSpec((m, 2 * D_MODEL), lambda i: (0, 0))],
        out_shape=[jax.ShapeDtypeStruct((m, D_MODEL), F32),
                   jax.ShapeDtypeStruct((m, 2 * D_MODEL), F32)],
        compiler_params=_params(1), name="conv_sample")(x, state, g, w_in, cw, w_out)


def _kv_rows(x, g, w, gk, cos, sin_lo, sin_hi):
    kv = _dot(_rms(x, g), w)
    k = _head_norm_rope(kv[:, :KV_DIM], gk, cos, sin_lo, sin_hi)
    return k, kv[:, KV_DIM:]


def _kv_prompt_body(x_ref, g_ref, w_ref, gk_ref, cos_ref, slo_ref, shi_ref, k_ref, v_ref, vt_ref):
    k, v = _kv_rows(x_ref[...], g_ref[...], w_ref[...], gk_ref[...],
                    cos_ref[...], slo_ref[...], shi_ref[...])
    k_ref[...] = k
    v_ref[...] = v
    vt_ref[...] = v.T


def _kv_prompt(x, g, w, gk, tables, tm):
    nb, s, _ = x.shape
    row = pl.BlockSpec((None, tm, D_MODEL), lambda b, t: (b, t, 0))
    tab = pl.BlockSpec((tm, LANES), lambda b, t: (t, 0))
    out = pl.BlockSpec((None, tm, KV_DIM), lambda b, t: (b, t, 0))
    out_t = pl.BlockSpec((None, KV_DIM, tm), lambda b, t: (b, 0, t))
    return pl.pallas_call(
        _kv_prompt_body, grid=(nb, s // tm),
        in_specs=[row, _const_spec((1, D_MODEL)), _const_spec((D_MODEL, 2 * KV_DIM)),
                  _const_spec((1, KV_DIM)), tab, tab, tab],
        out_specs=[out, out, out_t],
        out_shape=[jax.ShapeDtypeStruct((nb, s, KV_DIM), F32)] * 2
        + [jax.ShapeDtypeStruct((nb, KV_DIM, s), F32)],
        compiler_params=_params(2), name="kv_prompt")(x, g, w, gk, *tables)


def _kv_sample_body(x_ref, g_ref, w_ref, gk_ref, cos_ref, slo_ref, shi_ref, ck_ref, cv_ref,
                    k_ref, v_ref, nk_ref, nv_ref):
    k, v = _kv_rows(x_ref[...], g_ref[...], w_ref[...], gk_ref[...],
                    cos_ref[...], slo_ref[...], shi_ref[...])
    k_ref[...] = k
    v_ref[...] = v
    w_buf = ck_ref.shape[1]

    def shift(b, carry):
        nk_ref[b, 0:w_buf - 1, :] = ck_ref[b, 1:w_buf, :]
        nv_ref[b, 0:w_buf - 1, :] = cv_ref[b, 1:w_buf, :]
        nk_ref[b, w_buf - 1:w_buf, :] = k_ref[pl.ds(b, 1), :]
        nv_ref[b, w_buf - 1:w_buf, :] = v_ref[pl.ds(b, 1), :]
        return carry

    lax.fori_loop(0, x_ref.shape[0], shift, 0)


def _kv_sample(x, g, w, gk, tables, ck, cv, chunk):
    m = x.shape[0]
    w_buf = ck.shape[1]
    row = pl.BlockSpec((chunk, D_MODEL), lambda i: (i, 0))
    tab = pl.BlockSpec((chunk, LANES), lambda i: (i, 0))
    new = pl.BlockSpec((chunk, KV_DIM), lambda i: (i, 0))
    cache = pl.BlockSpec((chunk, w_buf, KV_DIM), lambda i: (i, 0, 0))
    return pl.pallas_call(
        _kv_sample_body, grid=(m // chunk,),
        in_specs=[row, _const_spec((1, D_MODEL)), _const_spec((D_MODEL, 2 * KV_DIM)),
                  _const_spec((1, KV_DIM)), tab, tab, tab, cache, cache],
        out_specs=[new, new, cache, cache],
        out_shape=[jax.ShapeDtypeStruct((m, KV_DIM), F32)] * 2
        + [jax.ShapeDtypeStruct(ck.shape, F32)] * 2,
        compiler_params=_params(1), name="kv_sample")(x, g, w, gk, *tables, ck, cv)


def _queries(x, g, wq, gq, cos, sin_lo, sin_hi):
    q = _dot(_rms(x, g), wq)
    return _head_norm_rope(q, gq, cos, sin_lo, sin_hi) * (1.0 / math.sqrt(HEAD_DIM))


def _attn_prompt_body(sinks_ref, x_ref, kc_ref, kp_ref, vtc_ref, vtp_ref, g_ref, wq_ref, gq_ref,
                      cos_ref, slo_ref, shi_ref, wo_ref, o_ref, q_scr, o_scr):
    tm = x_ref.shape[0]
    first_tile = pl.program_id(1) == 0
    x = x_ref[...]
    q_scr[...] = _queries(x, g_ref[...], wq_ref[...], gq_ref[...],
                          cos_ref[...], slo_ref[...], shi_ref[...]).astype(BF16)
    kcat = jnp.concatenate([kp_ref[...], kc_ref[...]], axis=0).astype(BF16)
    vtcat = jnp.concatenate([vtp_ref[...], vtc_ref[...]], axis=1).astype(BF16)

    seg = lax.broadcasted_iota(jnp.int32, (1, KV_DIM), 1) // HEAD_DIM
    kj = lax.broadcasted_iota(jnp.int32, (2 * WINDOW, WINDOW), 0)
    qi = lax.broadcasted_iota(jnp.int32, (2 * WINDOW, WINDOW), 1)
    rel = qi + WINDOW - kj
    band = (rel >= 0) & (rel < WINDOW)
    band0 = band & ((kj >= WINDOW) | jnp.logical_not(first_tile))
    bias = jnp.where(band, 0.0, -jnp.inf).astype(F32)
    bias0 = jnp.where(band0, 0.0, -jnp.inf).astype(F32)
    no_rows = jnp.zeros((HEAD_DIM, 2 * WINDOW), BF16)

    for i in range(tm // WINDOW):
        rows = slice(WINDOW * i, WINDOW * (i + 1))
        kw = kcat[WINDOW * i:WINDOW * (i + 2)]
        vtw = vtcat[:, WINDOW * i:WINDOW * (i + 2)]
        q4 = jnp.concatenate(
            [q_scr[rows, KV_DIM * g:KV_DIM * (g + 1)] for g in range(GROUP)], axis=0)
        bias4 = jnp.concatenate([bias0 if i == 0 else bias] * GROUP, axis=1)
        probs, vts = [], []
        for h in range(N_KV_HEADS):
            s = _dot_nt(jnp.where(seg == h, kw, jnp.zeros_like(kw)), q4) + bias4
            sink = jnp.concatenate(
                [jnp.full((1, WINDOW), sinks_ref[N_KV_HEADS * g + h], F32)
                 for g in range(GROUP)], axis=1)
            m = jnp.maximum(jnp.max(s, axis=0, keepdims=True), sink)
            p = jnp.exp(s - m)
            den = jnp.sum(p, axis=0, keepdims=True) + jnp.exp(sink - m)
            probs.append((p * (1.0 / den)).astype(BF16))
            vts.append(jnp.concatenate(
                [vtw[HEAD_DIM * h:HEAD_DIM * (h + 1)] if hh == h else no_rows
                 for hh in range(N_KV_HEADS)], axis=0))
        o_t = _dot(jnp.concatenate(vts, axis=1), jnp.concatenate(probs, axis=0))
        for g in range(GROUP):
            o_scr[rows, KV_DIM * g:KV_DIM * (g + 1)] = (
                o_t[:, WINDOW * g:WINDOW * (g + 1)].T)

    o_ref[...] = x + _dot(o_scr[...], wo_ref[...])


def _attn_prompt(x, k, vt, g, wq, gq, tables, sinks, wo, layer, tm):
    nb, s, _ = x.shape
    per_tile = tm // WINDOW
    row = pl.BlockSpec((None, tm, D_MODEL), lambda b, t: (b, t, 0))
    cur = pl.BlockSpec((None, tm, KV_DIM), lambda b, t: (b, t, 0))
    prev = pl.BlockSpec((None, WINDOW, KV_DIM),
                        lambda b, t: (b, jnp.maximum(t * per_tile - 1, 0), 0))
    cur_t = pl.BlockSpec((None, KV_DIM, tm), lambda b, t: (b, 0, t))
    prev_t = pl.BlockSpec((None, KV_DIM, WINDOW),
                          lambda b, t: (b, 0, jnp.maximum(t * per_tile - 1, 0)))
    tab = pl.BlockSpec((tm, LANES), lambda b, t: (t, 0))
    return pl.pallas_call(
        _attn_prompt_body, grid=(nb, s // tm),
        in_specs=[pl.BlockSpec(memory_space=pltpu.SMEM), row, cur, prev, cur_t, prev_t,
                  _const_spec((1, D_MODEL)), _layer_spec((D_MODEL, D_MODEL), layer),
                  _const_spec((1, D_MODEL)), tab, tab, tab, _layer_spec((D_MODEL, D_MODEL), layer)],
        out_specs=row,
        out_shape=jax.ShapeDtypeStruct(x.shape, F32),
        scratch_shapes=[pltpu.VMEM((tm, D_MODEL), BF16), pltpu.VMEM((tm, D_MODEL), F32)],
        compiler_params=_params(2), name="attn_prompt")(
            sinks, x, k, k, vt, vt, g, wq, gq, *tables, wo)


def _attn_sample_body(sinks_ref, x_ref, ck_ref, cv_ref, kn_ref, vn_ref, g_ref, wq_ref, gq_ref,
                      cos_ref, slo_ref, shi_ref, wo_ref, o_ref, q_scr, s_scr, p_scr, o_scr):
    chunk = x_ref.shape[0]
    x = x_ref[...]
    q_scr[...] = _queries(x, g_ref[...], wq_ref[...], gq_ref[...],
                          cos_ref[...], slo_ref[...], shi_ref[...])
    w_buf = ck_ref.shape[1]
    seg_rows = (lax.broadcasted_iota(jnp.int32, (N_KV_HEADS, KV_DIM), 1) // HEAD_DIM
                == lax.broadcasted_iota(jnp.int32, (N_KV_HEADS, KV_DIM), 0))
    expired = lax.broadcasted_iota(jnp.int32, (w_buf, 1), 0) == 0

    def scores(b, carry):
        kb = jnp.where(expired, kn_ref[pl.ds(b, 1), :], ck_ref[b]).astype(BF16)
        qrow = q_scr[pl.ds(b, 1), :]
        qrows = jnp.concatenate(
            [jnp.where(seg_rows, jnp.broadcast_to(qrow[:, KV_DIM * g:KV_DIM * (g + 1)],
                                                  (N_KV_HEADS, KV_DIM)), 0.0)
             for g in range(GROUP)], axis=0)
        s_scr[pl.ds(pl.multiple_of(b * N_HEADS, N_HEADS), N_HEADS), :] = _dot_nt(
            qrows.astype(BF16), kb)
        return carry

    lax.fori_loop(0, chunk, scores, 0, unroll=SAMPLE_UNROLL)

    s = s_scr[...]
    sink = jnp.concatenate([jnp.full((1, 1), sinks_ref[i], F32) for i in range(N_HEADS)] * chunk,
                           axis=0)
    m = jnp.maximum(jnp.max(s, axis=-1, keepdims=True), sink)
    p = jnp.exp(s - m)
    den = jnp.sum(p, axis=-1, keepdims=True) + jnp.exp(sink - m)
    p_scr[...] = (p * (1.0 / den)).astype(BF16)

    def outputs(b, carry):
        vb = jnp.where(expired, vn_ref[pl.ds(b, 1), :], cv_ref[b]).astype(BF16)
        r = _dot(p_scr[pl.ds(pl.multiple_of(b * N_HEADS, N_HEADS), N_HEADS), :], vb)
        o_scr[pl.ds(b, 1), :] = jnp.concatenate(
            [jnp.sum(jnp.where(seg_rows, r[N_KV_HEADS * g:N_KV_HEADS * (g + 1)], 0.0),
                     axis=0, keepdims=True) for g in range(GROUP)], axis=1)
        return carry

    lax.fori_loop(0, chunk, outputs, 0, unroll=SAMPLE_UNROLL)
    o_ref[...] = x + _dot(o_scr[...], wo_ref[...])


def _attn_sample(x, ck, cv, kn, vn, g, wq, gq, tables, sinks, wo, layer, chunk):
    m = x.shape[0]
    w_buf = ck.shape[1]
    row = pl.BlockSpec((chunk, D_MODEL), lambda i: (i, 0))
    tab = pl.BlockSpec((chunk, LANES), lambda i: (i, 0))
    new = pl.BlockSpec((chunk, KV_DIM), lambda i: (i, 0))
    cache = pl.BlockSpec((chunk, w_buf, KV_DIM), lambda i: (i, 0, 0))
    return pl.pallas_call(
        _attn_sample_body, grid=(m // chunk,),
        in_specs=[pl.BlockSpec(memory_space=pltpu.SMEM), row, cache, cache, new, new,
                  _const_spec((1, D_MODEL)), _layer_spec((D_MODEL, D_MODEL), layer),
                  _const_spec((1, D_MODEL)), tab, tab, tab, _layer_spec((D_MODEL, D_MODEL), layer)],
        out_specs=row,
        out_shape=jax.ShapeDtypeStruct(x.shape, F32),
        scratch_shapes=[pltpu.VMEM((chunk, D_MODEL), F32),
                        pltpu.VMEM((chunk * N_HEADS, w_buf), F32),
                        pltpu.VMEM((chunk * N_HEADS, w_buf), BF16),
                        pltpu.VMEM((chunk, D_MODEL), F32)],
        compiler_params=_params(1), name="attn_sample")(
            sinks, x, ck, cv, kn, vn, g, wq, gq, *tables, wo)


def _rope_tables(pos):
    half = ROT_DIM // 2
    inv_freq = ROPE_THETA ** (-jnp.arange(0, ROT_DIM, 2, dtype=F32) / ROT_DIM)
    ang = pos.astype(F32)[:, None] * inv_freq[None, :]
    cos, sin = jnp.cos(ang), jnp.sin(ang)
    n = pos.shape[0]
    rest = HEAD_DIM - ROT_DIM
    cos_h = jnp.concatenate([cos, cos, jnp.ones((n, rest), F32)], axis=1)
    sin_lo = jnp.concatenate([-sin, jnp.zeros((n, half + rest), F32)], axis=1)
    sin_hi = jnp.concatenate([jnp.zeros((n, half), F32), sin, jnp.zeros((n, rest), F32)], axis=1)
    reps = LANES // HEAD_DIM
    return tuple(jnp.tile(t, (1, reps)) for t in (cos_h, sin_lo, sin_hi))


def kernel(x_prompt, x_sample, state_conv, cache_k, cache_v, g_ffn1, w_ffn1_gate, w_ffn1_up,
           w_ffn1_down, g_mix, g_ffn2, w_ffn2_gate, w_ffn2_up, w_ffn2_down, w_in_a, conv_w,
           w_out_a, g_kv, w_kv, g_knorm, w_q, g_qnorm, sinks, w_o):
    nb, seq, _ = x_prompt.shape
    nd, dec_seq, _ = x_sample.shape
    assert dec_seq == 1 and cache_k.shape[1] == WINDOW
    depth = g_ffn1.shape[0]
    n_a = w_in_a.shape[0]
    past_len = seq
    w_buf = cache_k.shape[1]

    n_b = w_q.shape[0]
    ffn1 = (w_ffn1_gate, w_ffn1_up, w_ffn1_down)
    ffn2 = (w_ffn2_gate, w_ffn2_up, w_ffn2_down)
    w_in, w_out, wkv = w_in_a, w_out_a, w_kv
    wq = (w_q.reshape(n_b, D_MODEL, N_KV_HEADS, GROUP, HEAD_DIM).transpose(0, 1, 3, 2, 4)
          .reshape(n_b, D_MODEL, D_MODEL))
    wo = (w_o.reshape(n_b, N_KV_HEADS, GROUP, HEAD_DIM, D_MODEL).transpose(0, 2, 1, 3, 4)
          .reshape(n_b, D_MODEL, D_MODEL))
    sinks_gm = sinks.reshape(n_b, N_KV_HEADS, GROUP).transpose(0, 2, 1).reshape(n_b, N_HEADS)
    tab_p = _rope_tables(jnp.arange(seq, dtype=jnp.int32))
    tab_s = _rope_tables(jnp.full((nd,), past_len, jnp.int32))
    gk = jnp.tile(g_knorm, N_KV_HEADS)[None]

    xp = x_prompt
    xs = x_sample.reshape(nd, D_MODEL)
    ck = cache_k.reshape(nd, w_buf, KV_DIM)
    cv = cache_v.reshape(nd, w_buf, KV_DIM)
    conv_p, conv_s = [], []
    kp = vp = vtp = kn = vn = new_ck = new_cv = None

    def ffn_both(xp, xs, g, weights, layer):
        xp, xs = _ffn(xp.reshape(nb * seq, D_MODEL), xs, g[None], *weights, layer, TM_FFN)
        return xp.reshape(nb, seq, D_MODEL), xs

    for i in range(depth):
        if i == n_a:
            kp, vp, vtp = _kv_prompt(xp, g_kv[None], wkv, gk, tab_p, TM_MIX)
            kn, vn, new_ck, new_cv = _kv_sample(xs, g_kv[None], wkv, gk, tab_s, ck, cv,
                                                SAMPLE_CHUNK)
        xp, xs = ffn_both(xp, xs, g_ffn1[i], ffn1, i)
        gm = g_mix[i][None]
        if i < n_a:
            xp, st = _conv_prompt(xp, jnp.zeros((nb, CONV_W - 1, D_MODEL), F32), gm, w_in,
                                  conv_w[i], w_out, i, TM_MIX)
            conv_p.append(st)
            xs, st = _conv_sample(xs, state_conv[i].reshape(nd, (CONV_W - 1) * D_MODEL), gm,
                                  w_in, conv_w[i], w_out, i)
            conv_s.append(st.reshape(nd, CONV_W - 1, D_MODEL))
        else:
            j = i - n_a
            gq = jnp.tile(g_qnorm[j], N_HEADS)[None]
            xp = _attn_prompt(xp, kp, vtp, gm, wq, gq, tab_p, sinks_gm[j], wo, j, TM_MIX)
            xs = _attn_sample(xs, ck, cv, kn, vn, gm, wq, gq, tab_s, sinks_gm[j], wo, j,
                              SAMPLE_CHUNK)
        xp, xs = ffn_both(xp, xs, g_ffn2[i], ffn2, i)

    kv_shape = (w_buf, N_KV_HEADS, HEAD_DIM)
    return (xp, xs.reshape(nd, 1, D_MODEL), jnp.stack(conv_p),
            kp[:, seq - w_buf:].reshape(nb, *kv_shape), vp[:, seq - w_buf:].reshape(nb, *kv_shape),
            jnp.stack(conv_s), new_ck.reshape(nd, *kv_shape), new_cv.reshape(nd, *kv_shape))
```

```python
import functools
import math

import jax
import jax.numpy as jnp
from jax import lax
from jax.experimental import pallas as pl
from jax.experimental.pallas import tpu as pltpu

D_MODEL = 1024
D_FF = 2816
N_HEADS = 16
N_KV_HEADS = 4
GROUP = N_HEADS // N_KV_HEADS
HEAD_DIM = 64
KV_DIM = N_KV_HEADS * HEAD_DIM
ROT_DIM = HEAD_DIM // 4
ROPE_THETA = 500000.0
WINDOW = 128
EPS = 1e-6
LOG2E = math.log2(math.e)
CONV_W = 3

LANES = 128
SUBLANES = 8
VMEM_LIMIT = 56 * 1024 * 1024

FFN_CHUNK = 256
TM_FFN = 512
TM_CONV = 512
TM_KV = 1024
TM_ATTN = 1024
SAMPLE_CHUNK = 32
SAMPLE_UNROLL = 4

F32 = jnp.float32
BF16 = jnp.bfloat16


def _const_spec(shape):
    return pl.BlockSpec(shape, lambda *_: (0,) * len(shape), pipeline_mode=pl.Buffered(1))


def _layer_spec(shape, layer):
    return pl.BlockSpec((None,) + tuple(shape), lambda *_: (layer,) + (0,) * len(shape),
                        pipeline_mode=pl.Buffered(1))


def _params(n_axes):
    return pltpu.CompilerParams(dimension_semantics=("arbitrary",) * n_axes,
                                vmem_limit_bytes=VMEM_LIMIT)


def _dot(a, b):
    return jnp.dot(a, b, preferred_element_type=F32)


def _dot_nt(a, b):
    return lax.dot_general(a, b, (((1,), (1,)), ((), ())), preferred_element_type=F32)


def _rms(x, g):
    ms = jnp.mean(x * x, axis=-1, keepdims=True)
    return x * lax.rsqrt(ms + EPS) * g


def _head_mean_matrix():
    r = lax.broadcasted_iota(jnp.int32, (LANES, LANES), 0) // HEAD_DIM
    c = lax.broadcasted_iota(jnp.int32, (LANES, LANES), 1) // HEAD_DIM
    return jnp.where(r == c, 1.0 / HEAD_DIM, 0.0).astype(F32)


def _head_norm_rope(x, gain, cos, sin_lo, sin_hi):
    bd = _head_mean_matrix()
    outs = []
    for j in range(x.shape[1] // LANES):
        xs = x[:, LANES * j:LANES * (j + 1)]
        ms = _dot(xs * xs, bd)
        xn = xs * lax.rsqrt(ms + EPS) * gain[:, LANES * j:LANES * (j + 1)]
        outs.append(xn * cos + pltpu.roll(xn, ROT_DIM // 2, 1) * sin_hi
                    + pltpu.roll(xn, LANES - ROT_DIM // 2, 1) * sin_lo)
    return jnp.concatenate(outs, axis=1)


def _ffn_rows(x, g, wg_ref, wu_ref, wd_ref):
    h = _rms(x, g)
    acc = jnp.zeros_like(x)
    for c in range(D_FF // FFN_CHUNK):
        sl = slice(c * FFN_CHUNK, (c + 1) * FFN_CHUNK)
        gate = _dot(h, wg_ref[:, sl])
        up = _dot(h, wu_ref[:, sl])
        a = gate * jax.nn.sigmoid(gate) * up
        acc = acc + _dot(a, wd_ref[sl, :])
    return x + 0.5 * acc


def _ffn_body(xp_ref, xs_ref, g_ref, wg_ref, wu_ref, wd_ref, op_ref, os_ref):
    step = pl.program_id(0)
    last = pl.num_programs(0) - 1

    @pl.when(step < last)
    def _():
        op_ref[...] = _ffn_rows(xp_ref[...], g_ref[...], wg_ref, wu_ref, wd_ref)

    @pl.when(step == last)
    def _():
        os_ref[...] = _ffn_rows(xs_ref[...], g_ref[...], wg_ref, wu_ref, wd_ref)


def _ffn(xp, xs, g, wg, wu, wd, layer, tm):
    m, ms = xp.shape[0], xs.shape[0]
    n_p = m // tm
    row_p = pl.BlockSpec((tm, D_MODEL), lambda i: (jnp.minimum(i, n_p - 1), 0))
    return pl.pallas_call(
        _ffn_body, grid=(n_p + 1,),
        in_specs=[row_p, _const_spec((ms, D_MODEL)), _const_spec((1, D_MODEL)),
                  _layer_spec((D_MODEL, D_FF), layer), _layer_spec((D_MODEL, D_FF), layer),
                  _layer_spec((D_FF, D_MODEL), layer)],
        out_specs=[row_p, pl.BlockSpec((ms, D_MODEL), lambda i: (0, 0))],
        out_shape=[jax.ShapeDtypeStruct((m, D_MODEL), F32),
                   jax.ShapeDtypeStruct((ms, D_MODEL), F32)],
        compiler_params=_params(1), name="ffn")(xp, xs, g, wg, wu, wd)


def _conv_prompt_body(x_ref, st_ref, g_ref, win_ref, cw_ref, wout_ref, o_ref, nst_ref, ext_ref):
    tm = x_ref.shape[0]
    lead = SUBLANES - (CONV_W - 1)

    @pl.when(pl.program_id(1) == 0)
    def _():
        ext_ref[lead:SUBLANES, :] = st_ref[...]

    x = x_ref[...]
    bcu = _dot(_rms(x, g_ref[...]), win_ref[...])
    b = bcu[:, :D_MODEL]
    cu = bcu[:, D_MODEL:2 * D_MODEL] * bcu[:, 2 * D_MODEL:]
    ext_ref[SUBLANES:SUBLANES + tm, :] = cu
    cw = cw_ref[...]
    conv = (cw[0:1] * ext_ref[lead:lead + tm, :] + cw[1:2] * ext_ref[lead + 1:lead + 1 + tm, :]
            + cw[2:3] * cu)
    y = _dot(b * conv, wout_ref[...])
    o_ref[...] = x + y
    tail = ext_ref[lead + tm:SUBLANES + tm, :]
    ext_ref[lead:SUBLANES, :] = tail
    nst_ref[...] = tail


def _conv_prompt(x, state, g, w_in, cw, w_out, layer, tm):
    nb, s, _ = x.shape
    row = pl.BlockSpec((None, tm, D_MODEL), lambda b, t: (b, t, 0))
    st = pl.BlockSpec((None, CONV_W - 1, D_MODEL), lambda b, t: (b, 0, 0))
    return pl.pallas_call(
        _conv_prompt_body, grid=(nb, s // tm),
        in_specs=[row, st, _const_spec((1, D_MODEL)), _layer_spec((D_MODEL, 3 * D_MODEL), layer),
                  _const_spec((CONV_W, D_MODEL)), _layer_spec((D_MODEL, D_MODEL), layer)],
        out_specs=[row, st],
        out_shape=[jax.ShapeDtypeStruct(x.shape, F32),
                   jax.ShapeDtypeStruct((nb, CONV_W - 1, D_MODEL), F32)],
        scratch_shapes=[pltpu.VMEM((tm + SUBLANES, D_MODEL), F32)],
        compiler_params=_params(2), name="conv_prompt")(x, state, g, w_in, cw, w_out)


def _conv_sample_body(x_ref, st_ref, g_ref, win_ref, cw_ref, wout_ref, o_ref, nst_ref):
    x = x_ref[...]
    bcu = _dot(_rms(x, g_ref[...]), win_ref[...])
    b = bcu[:, :D_MODEL]
    cu = bcu[:, D_MODEL:2 * D_MODEL] * bcu[:, 2 * D_MODEL:]
    cw = cw_ref[...]
    s1 = st_ref[:, D_MODEL:]
    conv = cw[0:1] * st_ref[:, :D_MODEL] + cw[1:2] * s1 + cw[2:3] * cu
    o_ref[...] = x + _dot(b * conv, wout_ref[...])
    nst_ref[:, :D_MODEL] = s1
    nst_ref[:, D_MODEL:] = cu


def _conv_sample(x, state, g, w_in, cw, w_out, layer):
    m = x.shape[0]
    return pl.pallas_call(
        _conv_sample_body, grid=(1,),
        in_specs=[_const_spec((m, D_MODEL)), _const_spec((m, 2 * D_MODEL)),
                  _const_spec((1, D_MODEL)), _layer_spec((D_MODEL, 3 * D_MODEL), layer),
                  _const_spec((CONV_W, D_MODEL)), _layer_spec((D_MODEL, D_MODEL), layer)],
        out_specs=[pl.BlockSpec((m, D_MODEL), lambda i: (0, 0)),
                   pl.BlockSpec((m, 2 * D_MODEL), lambda i: (0, 0))],
        out_shape=[jax.ShapeDtypeStruct((m, D_MODEL), F32),
                   jax.ShapeDtypeStruct((m, 2 * D_MODEL), F32)],
        compiler_params=_params(1), name="conv_sample")(x, state, g, w_in, cw, w_out)


def _kv_rows(x, g, w, gk, cos, sin_lo, sin_hi):
    kv = _dot(_rms(x, g), w)
    k = _head_norm_rope(kv[:, :KV_DIM], gk, cos, sin_lo, sin_hi)
    return k, kv[:, KV_DIM:]


def _kv_prompt_body(x_ref, g_ref, w_ref, gk_ref, cos_ref, slo_ref, shi_ref, k_ref, v_ref, vt_ref):
    k, v = _kv_rows(x_ref[...], g_ref[...], w_ref[...], gk_ref[...],
                    cos_ref[...], slo_ref[...], shi_ref[...])
    k_ref[...] = k
    v_ref[...] = v
    vt_ref[...] = v.T


def _kv_prompt(x, g, w, gk, tables, tm):
    nb, s, _ = x.shape
    row = pl.BlockSpec((None, tm, D_MODEL), lambda b, t: (b, t, 0))
    tab = pl.BlockSpec((tm, LANES), lambda b, t: (t, 0))
    out = pl.BlockSpec((None, tm, KV_DIM), lambda b, t: (b, t, 0))
    out_t = pl.BlockSpec((None, KV_DIM, tm), lambda b, t: (b, 0, t))
    return pl.pallas_call(
        _kv_prompt_body, grid=(nb, s // tm),
        in_specs=[row, _const_spec((1, D_MODEL)), _const_spec((D_MODEL, 2 * KV_DIM)),
                  _const_spec((1, KV_DIM)), tab, tab, tab],
        out_specs=[out, out, out_t],
        out_shape=[jax.ShapeDtypeStruct((nb, s, KV_DIM), F32)] * 2
        + [jax.ShapeDtypeStruct((nb, KV_DIM, s), F32)],
        compiler_params=_params(2), name="kv_prompt")(x, g, w, gk, *tables)


def _kv_sample_body(x_ref, g_ref, w_ref, gk_ref, cos_ref, slo_ref, shi_ref, ck_ref, cv_ref,
                    k_ref, v_ref, nk_ref, nv_ref):
    k, v = _kv_rows(x_ref[...], g_ref[...], w_ref[...], gk_ref[...],
                    cos_ref[...], slo_ref[...], shi_ref[...])
    k_ref[...] = k
    v_ref[...] = v
    w_buf = ck_ref.shape[1]

    def shift(b, carry):
        nk_ref[b, 0:w_buf - 1, :] = ck_ref[b, 1:w_buf, :]
        nv_ref[b, 0:w_buf - 1, :] = cv_ref[b, 1:w_buf, :]
        nk_ref[b, w_buf - 1:w_buf, :] = k_ref[pl.ds(b, 1), :]
        nv_ref[b, w_buf - 1:w_buf, :] = v_ref[pl.ds(b, 1), :]
        return carry

    lax.fori_loop(0, x_ref.shape[0], shift, 0)


def _kv_sample(x, g, w, gk, tables, ck, cv, chunk):
    m = x.shape[0]
    w_buf = ck.shape[1]
    row = pl.BlockSpec((chunk, D_MODEL), lambda i: (i, 0))
    tab = pl.BlockSpec((chunk, LANES), lambda i: (i, 0))
    new = pl.BlockSpec((chunk, KV_DIM), lambda i: (i, 0))
    cache = pl.BlockSpec((chunk, w_buf, KV_DIM), lambda i: (i, 0, 0))
    return pl.pallas_call(
        _kv_sample_body, grid=(m // chunk,),
        in_specs=[row, _const_spec((1, D_MODEL)), _const_spec((D_MODEL, 2 * KV_DIM)),
                  _const_spec((1, KV_DIM)), tab, tab, tab, cache, cache],
        out_specs=[new, new, cache, cache],
        out_shape=[jax.ShapeDtypeStruct((m, KV_DIM), F32)] * 2
        + [jax.ShapeDtypeStruct(ck.shape, F32)] * 2,
        compiler_params=_params(1), name="kv_sample")(x, g, w, gk, *tables, ck, cv)


def _queries(x, g, wq, gq, cos, sin_lo, sin_hi):
    q = _dot(_rms(x, g), wq)
    return _head_norm_rope(q, gq * (LOG2E / math.sqrt(HEAD_DIM)), cos, sin_lo, sin_hi)


def _attn_prompt_body(sinks_ref, x_ref, kc_ref, kp_ref, vtc_ref, vtp_ref, g_ref, wq_ref, gq_ref,
                      cos_ref, slo_ref, shi_ref, wo_ref, o_ref, q_scr, o_scr):
    tm = x_ref.shape[0]
    first_tile = pl.program_id(1) == 0
    x = x_ref[...]
    q_scr[...] = _queries(x, g_ref[...], wq_ref[...], gq_ref[...],
                          cos_ref[...], slo_ref[...], shi_ref[...]).astype(BF16)
    kcat = jnp.concatenate([kp_ref[...], kc_ref[...]], axis=0).astype(BF16)
    vtcat = jnp.concatenate([vtp_ref[...], vtc_ref[...]], axis=1).astype(BF16)

    seg = lax.broadcasted_iota(jnp.int32, (1, KV_DIM), 1) // HEAD_DIM
    kj = lax.broadcasted_iota(jnp.int32, (2 * WINDOW, WINDOW), 0)
    qi = lax.broadcasted_iota(jnp.int32, (2 * WINDOW, WINDOW), 1)
    rel = qi + WINDOW - kj
    band = (rel >= 0) & (rel < WINDOW)
    band0 = band & ((kj >= WINDOW) | jnp.logical_not(first_tile))
    bias = jnp.where(band, 0.0, -jnp.inf).astype(F32)
    bias0 = jnp.where(band0, 0.0, -jnp.inf).astype(F32)
    no_rows = jnp.zeros((HEAD_DIM, 2 * WINDOW), BF16)

    for i in range(tm // WINDOW):
        rows = slice(WINDOW * i, WINDOW * (i + 1))
        kw = kcat[WINDOW * i:WINDOW * (i + 2)]
        vtw = vtcat[:, WINDOW * i:WINDOW * (i + 2)]
        q4 = jnp.concatenate(
            [q_scr[rows, KV_DIM * g:KV_DIM * (g + 1)] for g in range(GROUP)], axis=0)
        bias4 = jnp.concatenate([bias0 if i == 0 else bias] * GROUP, axis=1)
        probs, vts, rdens = [], [], []
        for h in range(N_KV_HEADS):
            s = _dot_nt(jnp.where(seg == h, kw, jnp.zeros_like(kw)), q4) + bias4
            sink = jnp.concatenate(
                [jnp.full((1, WINDOW), sinks_ref[N_KV_HEADS * g + h] * LOG2E, F32)
                 for g in range(GROUP)], axis=1)
            m = jnp.maximum(jnp.max(s, axis=0, keepdims=True), sink)
            p = jnp.exp2(s - m)
            rdens.append(1.0 / (jnp.sum(p, axis=0, keepdims=True) + jnp.exp2(sink - m)))
            probs.append(p.astype(BF16))
            vts.append(jnp.concatenate(
                [vtw[HEAD_DIM * h:HEAD_DIM * (h + 1)] if hh == h else no_rows
                 for hh in range(N_KV_HEADS)], axis=0))
        o_t = _dot(jnp.concatenate(vts, axis=1), jnp.concatenate(probs, axis=0))
        o_t = jnp.concatenate(
            [o_t[HEAD_DIM * h:HEAD_DIM * (h + 1)] * rdens[h] for h in range(N_KV_HEADS)], axis=0)
        for g in range(GROUP):
            o_scr[rows, KV_DIM * g:KV_DIM * (g + 1)] = (
                o_t[:, WINDOW * g:WINDOW * (g + 1)].T)

    o_ref[...] = x + _dot(o_scr[...], wo_ref[...])


def _attn_prompt(x, k, vt, g, wq, gq, tables, sinks, wo, layer, tm):
    nb, s, _ = x.shape
    per_tile = tm // WINDOW
    row = pl.BlockSpec((None, tm, D_MODEL), lambda b, t: (b, t, 0))
    cur = pl.BlockSpec((None, tm, KV_DIM), lambda b, t: (b, t, 0))
    prev = pl.BlockSpec((None, WINDOW, KV_DIM),
                        lambda b, t: (b, jnp.maximum(t * per_tile - 1, 0), 0))
    cur_t = pl.BlockSpec((None, KV_DIM, tm), lambda b, t: (b, 0, t))
    prev_t = pl.BlockSpec((None, KV_DIM, WINDOW),
                          lambda b, t: (b, 0, jnp.maximum(t * per_tile - 1, 0)))
    tab = pl.BlockSpec((tm, LANES), lambda b, t: (t, 0))
    return pl.pallas_call(
        _attn_prompt_body, grid=(nb, s // tm),
        in_specs=[pl.BlockSpec(memory_space=pltpu.SMEM), row, cur, prev, cur_t, prev_t,
                  _const_spec((1, D_MODEL)), _layer_spec((D_MODEL, D_MODEL), layer),
                  _const_spec((1, D_MODEL)), tab, tab, tab, _layer_spec((D_MODEL, D_MODEL), layer)],
        out_specs=row,
        out_shape=jax.ShapeDtypeStruct(x.shape, F32),
        scratch_shapes=[pltpu.VMEM((tm, D_MODEL), BF16), pltpu.VMEM((tm, D_MODEL), F32)],
        compiler_params=_params(2), name="attn_prompt")(
            sinks, x, k, k, vt, vt, g, wq, gq, *tables, wo)


def _attn_sample_body(sinks_ref, x_ref, ck_ref, cv_ref, kn_ref, vn_ref, g_ref, wq_ref, gq_ref,
                      cos_ref, slo_ref, shi_ref, wo_ref, o_ref, q_scr, s_scr, p_scr, o_scr):
    chunk = x_ref.shape[0]
    x = x_ref[...]
    q_scr[...] = _queries(x, g_ref[...], wq_ref[...], gq_ref[...],
                          cos_ref[...], slo_ref[...], shi_ref[...])
    w_buf = ck_ref.shape[1]
    seg_rows = (lax.broadcasted_iota(jnp.int32, (N_KV_HEADS, KV_DIM), 1) // HEAD_DIM
                == lax.broadcasted_iota(jnp.int32, (N_KV_HEADS, KV_DIM), 0))
    expired = lax.broadcasted_iota(jnp.int32, (w_buf, 1), 0) == 0

    def scores(b, carry):
        kb = jnp.where(expired, kn_ref[pl.ds(b, 1), :], ck_ref[b]).astype(BF16)
        qrow = q_scr[pl.ds(b, 1), :]
        qrows = jnp.concatenate(
            [jnp.where(seg_rows, jnp.broadcast_to(qrow[:, KV_DIM * g:KV_DIM * (g + 1)],
                                                  (N_KV_HEADS, KV_DIM)), 0.0)
             for g in range(GROUP)], axis=0)
        s_scr[pl.ds(pl.multiple_of(b * N_HEADS, N_HEADS), N_HEADS), :] = _dot_nt(
            qrows.astype(BF16), kb)
        return carry

    lax.fori_loop(0, chunk, scores, 0, unroll=SAMPLE_UNROLL)

    s = s_scr[...]
    sink = jnp.concatenate(
        [jnp.full((1, 1), sinks_ref[i] * LOG2E, F32) for i in range(N_HEADS)] * chunk, axis=0)
    m = jnp.maximum(jnp.max(s, axis=-1, keepdims=True), sink)
    p = jnp.exp2(s - m)
    den = jnp.sum(p, axis=-1, keepdims=True) + jnp.exp2(sink - m)
    p_scr[...] = (p * (1.0 / den)).astype(BF16)

    def outputs(b, carry):
        vb = jnp.where(expired, vn_ref[pl.ds(b, 1), :], cv_ref[b]).astype(BF16)
        r = _dot(p_scr[pl.ds(pl.multiple_of(b * N_HEADS, N_HEADS), N_HEADS), :], vb)
        o_scr[pl.ds(b, 1), :] = jnp.concatenate(
            [jnp.sum(jnp.where(seg_rows, r[N_KV_HEADS * g:N_KV_HEADS * (g + 1)], 0.0),
                     axis=0, keepdims=True) for g in range(GROUP)], axis=1)
        return carry

    lax.fori_loop(0, chunk, outputs, 0, unroll=SAMPLE_UNROLL)
    o_ref[...] = x + _dot(o_scr[...], wo_ref[...])


def _attn_sample(x, ck, cv, kn, vn, g, wq, gq, tables, sinks, wo, layer, chunk):
    m = x.shape[0]
    w_buf = ck.shape[1]
    row = pl.BlockSpec((chunk, D_MODEL), lambda i: (i, 0))
    tab = pl.BlockSpec((chunk, LANES), lambda i: (i, 0))
    new = pl.BlockSpec((chunk, KV_DIM), lambda i: (i, 0))
    cache = pl.BlockSpec((chunk, w_buf, KV_DIM), lambda i: (i, 0, 0))
    return pl.pallas_call(
        _attn_sample_body, grid=(m // chunk,),
        in_specs=[pl.BlockSpec(memory_space=pltpu.SMEM), row, cache, cache, new, new,
                  _const_spec((1, D_MODEL)), _layer_spec((D_MODEL, D_MODEL), layer),
                  _const_spec((1, D_MODEL)), tab, tab, tab, _layer_spec((D_MODEL, D_MODEL), layer)],
        out_specs=row,
        out_shape=jax.ShapeDtypeStruct(x.shape, F32),
        scratch_shapes=[pltpu.VMEM((chunk, D_MODEL), F32),
                        pltpu.VMEM((chunk * N_HEADS, w_buf), F32),
                        pltpu.VMEM((chunk * N_HEADS, w_buf), BF16),
                        pltpu.VMEM((chunk, D_MODEL), F32)],
        compiler_params=_params(1), name="attn_sample")(
            sinks, x, ck, cv, kn, vn, g, wq, gq, *tables, wo)


def _rope_tables(pos):
    half = ROT_DIM // 2
    inv_freq = ROPE_THETA ** (-jnp.arange(0, ROT_DIM, 2, dtype=F32) / ROT_DIM)
    ang = pos.astype(F32)[:, None] * inv_freq[None, :]
    cos, sin = jnp.cos(ang), jnp.sin(ang)
    n = pos.shape[0]
    rest = HEAD_DIM - ROT_DIM
    cos_h = jnp.concatenate([cos, cos, jnp.ones((n, rest), F32)], axis=1)
    sin_lo = jnp.concatenate([-sin, jnp.zeros((n, half + rest), F32)], axis=1)
    sin_hi = jnp.concatenate([jnp.zeros((n, half), F32), sin, jnp.zeros((n, rest), F32)], axis=1)
    reps = LANES // HEAD_DIM
    return tuple(jnp.tile(t, (1, reps)) for t in (cos_h, sin_lo, sin_hi))


def kernel(x_prompt, x_sample, state_conv, cache_k, cache_v, g_ffn1, w_ffn1_gate, w_ffn1_up,
           w_ffn1_down, g_mix, g_ffn2, w_ffn2_gate, w_ffn2_up, w_ffn2_down, w_in_a, conv_w,
           w_out_a, g_kv, w_kv, g_knorm, w_q, g_qnorm, sinks, w_o):
    nb, seq, _ = x_prompt.shape
    nd, dec_seq, _ = x_sample.shape
    assert dec_seq == 1 and cache_k.shape[1] == WINDOW
    depth = g_ffn1.shape[0]
    n_a = w_in_a.shape[0]
    past_len = seq
    w_buf = cache_k.shape[1]

    n_b = w_q.shape[0]
    ffn1 = (w_ffn1_gate, w_ffn1_up, w_ffn1_down)
    ffn2 = (w_ffn2_gate, w_ffn2_up, w_ffn2_down)
    w_in, w_out, wkv = w_in_a, w_out_a, w_kv
    wq = (w_q.reshape(n_b, D_MODEL, N_KV_HEADS, GROUP, HEAD_DIM).transpose(0, 1, 3, 2, 4)
          .reshape(n_b, D_MODEL, D_MODEL))
    wo = (w_o.reshape(n_b, N_KV_HEADS, GROUP, HEAD_DIM, D_MODEL).transpose(0, 2, 1, 3, 4)
          .reshape(n_b, D_MODEL, D_MODEL))
    sinks_gm = sinks.reshape(n_b, N_KV_HEADS, GROUP).transpose(0, 2, 1).reshape(n_b, N_HEADS)
    tab_p = _rope_tables(jnp.arange(seq, dtype=jnp.int32))
    tab_s = _rope_tables(jnp.full((nd,), past_len, jnp.int32))
    gk = jnp.tile(g_knorm, N_KV_HEADS)[None]

    xp = x_prompt
    xs = x_sample.reshape(nd, D_MODEL)
    ck = cache_k.reshape(nd, w_buf, KV_DIM)
    cv = cache_v.reshape(nd, w_buf, KV_DIM)
    conv_p, conv_s = [], []
    kp = vp = vtp = kn = vn = new_ck = new_cv = None

    def ffn_both(xp, xs, g, weights, layer):
        xp, xs = _ffn(xp.reshape(nb * seq, D_MODEL), xs, g[None], *weights, layer, TM_FFN)
        return xp.reshape(nb, seq, D_MODEL), xs

    for i in range(depth):
        if i == n_a:
            kp, vp, vtp = _kv_prompt(xp, g_kv[None], wkv, gk, tab_p, TM_KV)
            kn, vn, new_ck, new_cv = _kv_sample(xs, g_kv[None], wkv, gk, tab_s, ck, cv,
                                                SAMPLE_CHUNK)
        xp, xs = ffn_both(xp, xs, g_ffn1[i], ffn1, i)
        gm = g_mix[i][None]
        if i < n_a:
            xp, st = _conv_prompt(xp, jnp.zeros((nb, CONV_W - 1, D_MODEL), F32), gm, w_in,
                                  conv_w[i], w_out, i, TM_CONV)
            conv_p.append(st)
            xs, st = _conv_sample(xs, state_conv[i].reshape(nd, (CONV_W - 1) * D_MODEL), gm,
                                  w_in, conv_w[i], w_out, i)
            conv_s.append(st.reshape(nd, CONV_W - 1, D_MODEL))
        else:
            j = i - n_a
            gq = jnp.tile(g_qnorm[j], N_HEADS)[None]
            xp = _attn_prompt(xp, kp, vtp, gm, wq, gq, tab_p, sinks_gm[j], wo, j, TM_ATTN)
            xs = _attn_sample(xs, ck, cv, kn, vn, gm, wq, gq, tab_s, sinks_gm[j], wo, j,
                              SAMPLE_CHUNK)
        xp, xs = ffn_both(xp, xs, g_ffn2[i], ffn2, i)

    kv_shape = (w_buf, N_KV_HEADS, HEAD_DIM)
    return (xp, xs.reshape(nd, 1, D_MODEL), jnp.stack(conv_p),
            kp[:, seq - w_buf:].reshape(nb, *kv_shape), vp[:, seq - w_buf:].reshape(nb, *kv_shape),
            jnp.stack(conv_s), new_ck.reshape(nd, *kv_shape), new_cv.reshape(nd, *kv_shape))
```

```python
import functools
import math

import jax
import jax.numpy as jnp
from jax import lax
from jax.experimental import pallas as pl
from jax.experimental.pallas import tpu as pltpu

D_MODEL = 1024
D_FF = 2816
N_HEADS = 16
N_KV_HEADS = 4
GROUP = N_HEADS // N_KV_HEADS
HEAD_DIM = 64
KV_DIM = N_KV_HEADS * HEAD_DIM
ROT_DIM = HEAD_DIM // 4
ROPE_THETA = 500000.0
WINDOW = 128
EPS = 1e-6
LOG2E = math.log2(math.e)
CONV_W = 3

LANES = 128
SUBLANES = 8
VMEM_LIMIT = 56 * 1024 * 1024

FFN_CHUNK = 256
TM_FFN = 512
TM_CONV = 512
TM_KV = 1024
TM_ATTN = 1024
SAMPLE_CHUNK = 32
SAMPLE_UNROLL = 4

F32 = jnp.float32
BF16 = jnp.bfloat16


def _const_spec(shape):
    return pl.BlockSpec(shape, lambda *_: (0,) * len(shape), pipeline_mode=pl.Buffered(1))


def _layer_spec(shape, layer):
    return pl.BlockSpec((None,) + tuple(shape), lambda *_: (layer,) + (0,) * len(shape),
                        pipeline_mode=pl.Buffered(1))


def _params(n_axes):
    return pltpu.CompilerParams(dimension_semantics=("arbitrary",) * n_axes,
                                vmem_limit_bytes=VMEM_LIMIT)


def _dot(a, b):
    return jnp.dot(a, b, preferred_element_type=F32)


def _dot_nt(a, b):
    return lax.dot_general(a, b, (((1,), (1,)), ((), ())), preferred_element_type=F32)


def _rms(x, g):
    ms = jnp.mean(x * x, axis=-1, keepdims=True)
    return x * lax.rsqrt(ms + EPS) * g


def _head_mean_matrix():
    r = lax.broadcasted_iota(jnp.int32, (LANES, LANES), 0) // HEAD_DIM
    c = lax.broadcasted_iota(jnp.int32, (LANES, LANES), 1) // HEAD_DIM
    return jnp.where(r == c, 1.0 / HEAD_DIM, 0.0).astype(F32)


def _head_norm_rope(x, gain, cos, sin_lo, sin_hi):
    bd = _head_mean_matrix()
    outs = []
    for j in range(x.shape[1] // LANES):
        xs = x[:, LANES * j:LANES * (j + 1)]
        ms = _dot(xs * xs, bd)
        xn = xs * lax.rsqrt(ms + EPS) * gain[:, LANES * j:LANES * (j + 1)]
        outs.append(xn * cos + pltpu.roll(xn, ROT_DIM // 2, 1) * sin_hi
                    + pltpu.roll(xn, LANES - ROT_DIM // 2, 1) * sin_lo)
    return jnp.concatenate(outs, axis=1)


def _ffn_rows(x, g, wg_ref, wu_ref, wd_ref, before_chunk=None):
    h = _rms(x, g)
    acc = jnp.zeros_like(x)
    for c in range(D_FF // FFN_CHUNK):
        if before_chunk is not None:
            before_chunk(c)
        sl = slice(c * FFN_CHUNK, (c + 1) * FFN_CHUNK)
        gate = _dot(h, wg_ref[:, sl])
        up = _dot(h, wu_ref[:, sl])
        a = gate * jax.nn.sigmoid(gate) * up
        acc = acc + _dot(a, wd_ref[sl, :])
    return x + 0.5 * acc


def _ffn_body(xp_ref, xs_ref, g_ref, wg_hbm, wu_hbm, wd_hbm, op_ref, os_ref,
              wg_ref, wu_ref, wd_ref, sem, *, layer):
    step = pl.program_id(0)
    last = pl.num_programs(0) - 1

    def chunk_copies(c):
        cols = pl.ds(c * FFN_CHUNK, FFN_CHUNK)
        return (pltpu.make_async_copy(wg_hbm.at[layer, :, cols], wg_ref.at[:, cols], sem.at[0, c]),
                pltpu.make_async_copy(wu_hbm.at[layer, :, cols], wu_ref.at[:, cols], sem.at[1, c]),
                pltpu.make_async_copy(wd_hbm.at[layer, cols, :], wd_ref.at[cols, :], sem.at[2, c]))

    def wait_chunk(c):
        for copy in chunk_copies(c):
            copy.wait()

    @pl.when(step == 0)
    def _():
        for c in range(D_FF // FFN_CHUNK):
            for copy in chunk_copies(c):
                copy.start()
        op_ref[...] = _ffn_rows(xp_ref[...], g_ref[...], wg_ref, wu_ref, wd_ref, wait_chunk)

    @pl.when((step > 0) & (step < last))
    def _():
        op_ref[...] = _ffn_rows(xp_ref[...], g_ref[...], wg_ref, wu_ref, wd_ref)

    @pl.when(step == last)
    def _():
        os_ref[...] = _ffn_rows(xs_ref[...], g_ref[...], wg_ref, wu_ref, wd_ref)


def _ffn(xp, xs, g, wg, wu, wd, layer, tm):
    m, ms = xp.shape[0], xs.shape[0]
    n_p = m // tm
    assert n_p >= 1 and D_FF % FFN_CHUNK == 0
    row_p = pl.BlockSpec((tm, D_MODEL), lambda i: (jnp.minimum(i, n_p - 1), 0))
    hbm = pl.BlockSpec(memory_space=pl.ANY)
    return pl.pallas_call(
        functools.partial(_ffn_body, layer=layer), grid=(n_p + 1,),
        in_specs=[row_p, _const_spec((ms, D_MODEL)), _const_spec((1, D_MODEL)), hbm, hbm, hbm],
        out_specs=[row_p, pl.BlockSpec((ms, D_MODEL), lambda i: (0, 0))],
        out_shape=[jax.ShapeDtypeStruct((m, D_MODEL), F32),
                   jax.ShapeDtypeStruct((ms, D_MODEL), F32)],
        scratch_shapes=[pltpu.VMEM((D_MODEL, D_FF), F32), pltpu.VMEM((D_MODEL, D_FF), F32),
                        pltpu.VMEM((D_FF, D_MODEL), F32),
                        pltpu.SemaphoreType.DMA((3, D_FF // FFN_CHUNK))],
        compiler_params=_params(1), name="ffn")(xp, xs, g, wg, wu, wd)


def _conv_prompt_body(x_ref, st_ref, g_ref, win_ref, cw_ref, wout_ref, o_ref, nst_ref, ext_ref):
    tm = x_ref.shape[0]
    lead = SUBLANES - (CONV_W - 1)

    @pl.when(pl.program_id(1) == 0)
    def _():
        ext_ref[lead:SUBLANES, :] = st_ref[...]

    x = x_ref[...]
    bcu = _dot(_rms(x, g_ref[...]), win_ref[...])
    b = bcu[:, :D_MODEL]
    cu = bcu[:, D_MODEL:2 * D_MODEL] * bcu[:, 2 * D_MODEL:]
    ext_ref[SUBLANES:SUBLANES + tm, :] = cu
    cw = cw_ref[...]
    conv = (cw[0:1] * ext_ref[lead:lead + tm, :] + cw[1:2] * ext_ref[lead + 1:lead + 1 + tm, :]
            + cw[2:3] * cu)
    y = _dot(b * conv, wout_ref[...])
    o_ref[...] = x + y
    tail = ext_ref[lead + tm:SUBLANES + tm, :]
    ext_ref[lead:SUBLANES, :] = tail
    nst_ref[...] = tail


def _conv_prompt(x, state, g, w_in, cw, w_out, layer, tm):
    nb, s, _ = x.shape
    row = pl.BlockSpec((None, tm, D_MODEL), lambda b, t: (b, t, 0))
    st = pl.BlockSpec((None, CONV_W - 1, D_MODEL), lambda b, t: (b, 0, 0))
    return pl.pallas_call(
        _conv_prompt_body, grid=(nb, s // tm),
        in_specs=[row, st, _const_spec((1, D_MODEL)), _layer_spec((D_MODEL, 3 * D_MODEL), layer),
                  _const_spec((CONV_W, D_MODEL)), _layer_spec((D_MODEL, D_MODEL), layer)],
        out_specs=[row, st],
        out_shape=[jax.ShapeDtypeStruct(x.shape, F32),
                   jax.ShapeDtypeStruct((nb, CONV_W - 1, D_MODEL), F32)],
        scratch_shapes=[pltpu.VMEM((tm + SUBLANES, D_MODEL), F32)],
        compiler_params=_params(2), name="conv_prompt")(x, state, g, w_in, cw, w_out)


def _conv_sample_body(x_ref, st_ref, g_ref, win_ref, cw_ref, wout_ref, o_ref, nst_ref):
    x = x_ref[...]
    bcu = _dot(_rms(x, g_ref[...]), win_ref[...])
    b = bcu[:, :D_MODEL]
    cu = bcu[:, D_MODEL:2 * D_MODEL] * bcu[:, 2 * D_MODEL:]
    cw = cw_ref[...]
    s1 = st_ref[:, D_MODEL:]
    conv = cw[0:1] * st_ref[:, :D_MODEL] + cw[1:2] * s1 + cw[2:3] * cu
    o_ref[...] = x + _dot(b * conv, wout_ref[...])
    nst_ref[:, :D_MODEL] = s1
    nst_ref[:, D_MODEL:] = cu


def _conv_sample(x, state, g, w_in, cw, w_out, layer):
    m = x.shape[0]
    return pl.pallas_call(
        _conv_sample_body, grid=(1,),
        in_specs=[_const_spec((m, D_MODEL)), _const_spec((m, 2 * D_MODEL)),
                  _const_spec((1, D_MODEL)), _layer_spec((D_MODEL, 3 * D_MODEL), layer),
                  _const_spec((CONV_W, D_MODEL)), _layer_spec((D_MODEL, D_MODEL), layer)],
        out_specs=[pl.BlockSpec((m, D_MODEL), lambda i: (0, 0)),
                   pl.BlockSpec((m, 2 * D_MODEL), lambda i: (0, 0))],
        out_shape=[jax.ShapeDtypeStruct((m, D_MODEL), F32),
                   jax.ShapeDtypeStruct((m, 2 * D_MODEL), F32)],
        compiler_params=_params(1), name="conv_sample")(x, state, g, w_in, cw, w_out)


def _kv_rows(x, g, w, gk, cos, sin_lo, sin_hi):
    kv = _dot(_rms(x, g), w)
    k = _head_norm_rope(kv[:, :KV_DIM], gk, cos, sin_lo, sin_hi)
    return k, kv[:, KV_DIM:]


def _kv_prompt_body(x_ref, g_ref, w_ref, gk_ref, cos_ref, slo_ref, shi_ref, k_ref, v_ref, vt_ref):
    k, v = _kv_rows(x_ref[...], g_ref[...], w_ref[...], gk_ref[...],
                    cos_ref[...], slo_ref[...], shi_ref[...])
    k_ref[...] = k
    v_ref[...] = v
    vt_ref[...] = v.T


def _kv_prompt(x, g, w, gk, tables, tm):
    nb, s, _ = x.shape
    row = pl.BlockSpec((None, tm, D_MODEL), lambda b, t: (b, t, 0))
    tab = pl.BlockSpec((tm, LANES), lambda b, t: (t, 0))
    out = pl.BlockSpec((None, tm, KV_DIM), lambda b, t: (b, t, 0))
    out_t = pl.BlockSpec((None, KV_DIM, tm), lambda b, t: (b, 0, t))
    return pl.pallas_call(
        _kv_prompt_body, grid=(nb, s // tm),
        in_specs=[row, _const_spec((1, D_MODEL)), _const_spec((D_MODEL, 2 * KV_DIM)),
                  _const_spec((1, KV_DIM)), tab, tab, tab],
        out_specs=[out, out, out_t],
        out_shape=[jax.ShapeDtypeStruct((nb, s, KV_DIM), F32)] * 2
        + [jax.ShapeDtypeStruct((nb, KV_DIM, s), F32)],
        compiler_params=_params(2), name="kv_prompt")(x, g, w, gk, *tables)


def _kv_sample_body(x_ref, g_ref, w_ref, gk_ref, cos_ref, slo_ref, shi_ref, ck_ref, cv_ref,
                    k_ref, v_ref, nk_ref, nv_ref):
    k, v = _kv_rows(x_ref[...], g_ref[...], w_ref[...], gk_ref[...],
                    cos_ref[...], slo_ref[...], shi_ref[...])
    k_ref[...] = k
    v_ref[...] = v
    w_buf = ck_ref.shape[1]

    def shift(b, carry):
        nk_ref[b, 0:w_buf - 1, :] = ck_ref[b, 1:w_buf, :]
        nv_ref[b, 0:w_buf - 1, :] = cv_ref[b, 1:w_buf, :]
        nk_ref[b, w_buf - 1:w_buf, :] = k_ref[pl.ds(b, 1), :]
        nv_ref[b, w_buf - 1:w_buf, :] = v_ref[pl.ds(b, 1), :]
        return carry

    lax.fori_loop(0, x_ref.shape[0], shift, 0)


def _kv_sample(x, g, w, gk, tables, ck, cv, chunk):
    m = x.shape[0]
    w_buf = ck.shape[1]
    row = pl.BlockSpec((chunk, D_MODEL), lambda i: (i, 0))
    tab = pl.BlockSpec((chunk, LANES), lambda i: (i, 0))
    new = pl.BlockSpec((chunk, KV_DIM), lambda i: (i, 0))
    cache = pl.BlockSpec((chunk, w_buf, KV_DIM), lambda i: (i, 0, 0))
    return pl.pallas_call(
        _kv_sample_body, grid=(m // chunk,),
        in_specs=[row, _const_spec((1, D_MODEL)), _const_spec((D_MODEL, 2 * KV_DIM)),
                  _const_spec((1, KV_DIM)), tab, tab, tab, cache, cache],
        out_specs=[new, new, cache, cache],
        out_shape=[jax.ShapeDtypeStruct((m, KV_DIM), F32)] * 2
        + [jax.ShapeDtypeStruct(ck.shape, F32)] * 2,
        compiler_params=_params(1), name="kv_sample")(x, g, w, gk, *tables, ck, cv)


def _queries(x, g, wq, gq, cos, sin_lo, sin_hi):
    q = _dot(_rms(x, g), wq)
    return _head_norm_rope(q, gq * (LOG2E / math.sqrt(HEAD_DIM)), cos, sin_lo, sin_hi)


def _attn_prompt_body(sinks_ref, x_ref, kc_ref, kp_ref, vtc_ref, vtp_ref, g_ref, wq_ref, gq_ref,
                      cos_ref, slo_ref, shi_ref, wo_ref, o_ref, q_scr, o_scr):
    tm = x_ref.shape[0]
    first_tile = pl.program_id(1) == 0
    x = x_ref[...]
    q_scr[...] = _queries(x, g_ref[...], wq_ref[...], gq_ref[...],
                          cos_ref[...], slo_ref[...], shi_ref[...]).astype(BF16)
    kcat = jnp.concatenate([kp_ref[...], kc_ref[...]], axis=0).astype(BF16)
    vtcat = jnp.concatenate([vtp_ref[...], vtc_ref[...]], axis=1).astype(BF16)

    seg = lax.broadcasted_iota(jnp.int32, (1, KV_DIM), 1) // HEAD_DIM
    kj = lax.broadcasted_iota(jnp.int32, (2 * WINDOW, WINDOW), 0)
    qi = lax.broadcasted_iota(jnp.int32, (2 * WINDOW, WINDOW), 1)
    rel = qi + WINDOW - kj
    band = (rel >= 0) & (rel < WINDOW)
    band0 = band & ((kj >= WINDOW) | jnp.logical_not(first_tile))
    bias = jnp.where(band, 0.0, -jnp.inf).astype(F32)
    bias0 = jnp.where(band0, 0.0, -jnp.inf).astype(F32)
    no_rows = jnp.zeros((HEAD_DIM, 2 * WINDOW), BF16)

    for i in range(tm // WINDOW):
        rows = slice(WINDOW * i, WINDOW * (i + 1))
        kw = kcat[WINDOW * i:WINDOW * (i + 2)]
        vtw = vtcat[:, WINDOW * i:WINDOW * (i + 2)]
        q4 = jnp.concatenate(
            [q_scr[rows, KV_DIM * g:KV_DIM * (g + 1)] for g in range(GROUP)], axis=0)
        bias4 = jnp.concatenate([bias0 if i == 0 else bias] * GROUP, axis=1)
        probs, vts, rdens = [], [], []
        for h in range(N_KV_HEADS):
            s = _dot_nt(jnp.where(seg == h, kw, jnp.zeros_like(kw)), q4) + bias4
            sink = jnp.concatenate(
                [jnp.full((1, WINDOW), sinks_ref[N_KV_HEADS * g + h] * LOG2E, F32)
                 for g in range(GROUP)], axis=1)
            m = jnp.maximum(jnp.max(s, axis=0, keepdims=True), sink)
            p = jnp.exp2(s - m)
            rdens.append(1.0 / (jnp.sum(p, axis=0, keepdims=True) + jnp.exp2(sink - m)))
            probs.append(p.astype(BF16))
            vts.append(jnp.concatenate(
                [vtw[HEAD_DIM * h:HEAD_DIM * (h + 1)] if hh == h else no_rows
                 for hh in range(N_KV_HEADS)], axis=0))
        o_t = _dot(jnp.concatenate(vts, axis=1), jnp.concatenate(probs, axis=0))
        o_t = jnp.concatenate(
            [o_t[HEAD_DIM * h:HEAD_DIM * (h + 1)] * rdens[h] for h in range(N_KV_HEADS)], axis=0)
        for g in range(GROUP):
            o_scr[rows, KV_DIM * g:KV_DIM * (g + 1)] = (
                o_t[:, WINDOW * g:WINDOW * (g + 1)].T)

    o_ref[...] = x + _dot(o_scr[...], wo_ref[...])


def _attn_prompt(x, k, vt, g, wq, gq, tables, sinks, wo, layer, tm):
    nb, s, _ = x.shape
    per_tile = tm // WINDOW
    row = pl.BlockSpec((None, tm, D_MODEL), lambda b, t: (b, t, 0))
    cur = pl.BlockSpec((None, tm, KV_DIM), lambda b, t: (b, t, 0))
    prev = pl.BlockSpec((None, WINDOW, KV_DIM),
                        lambda b, t: (b, jnp.maximum(t * per_tile - 1, 0), 0))
    cur_t = pl.BlockSpec((None, KV_DIM, tm), lambda b, t: (b, 0, t))
    prev_t = pl.BlockSpec((None, KV_DIM, WINDOW),
                          lambda b, t: (b, 0, jnp.maximum(t * per_tile - 1, 0)))
    tab = pl.BlockSpec((tm, LANES), lambda b, t: (t, 0))
    return pl.pallas_call(
        _attn_prompt_body, grid=(nb, s // tm),
        in_specs=[pl.BlockSpec(memory_space=pltpu.SMEM), row, cur, prev, cur_t, prev_t,
                  _const_spec((1, D_MODEL)), _layer_spec((D_MODEL, D_MODEL), layer),
                  _const_spec((1, D_MODEL)), tab, tab, tab, _layer_spec((D_MODEL, D_MODEL), layer)],
        out_specs=row,
        out_shape=jax.ShapeDtypeStruct(x.shape, F32),
        scratch_shapes=[pltpu.VMEM((tm, D_MODEL), BF16), pltpu.VMEM((tm, D_MODEL), F32)],
        compiler_params=_params(2), name="attn_prompt")(
            sinks, x, k, k, vt, vt, g, wq, gq, *tables, wo)


def _attn_sample_body(sinks_ref, x_ref, ck_ref, cv_ref, kn_ref, vn_ref, g_ref, wq_ref, gq_ref,
                      cos_ref, slo_ref, shi_ref, wo_ref, o_ref, q_scr, s_scr, p_scr, o_scr):
    chunk = x_ref.shape[0]
    x = x_ref[...]
    q_scr[...] = _queries(x, g_ref[...], wq_ref[...], gq_ref[...],
                          cos_ref[...], slo_ref[...], shi_ref[...])
    w_buf = ck_ref.shape[1]
    seg_rows = (lax.broadcasted_iota(jnp.int32, (N_KV_HEADS, KV_DIM), 1) // HEAD_DIM
                == lax.broadcasted_iota(jnp.int32, (N_KV_HEADS, KV_DIM), 0))
    expired = lax.broadcasted_iota(jnp.int32, (w_buf, 1), 0) == 0

    def scores(b, carry):
        kb = jnp.where(expired, kn_ref[pl.ds(b, 1), :], ck_ref[b]).astype(BF16)
        qrow = q_scr[pl.ds(b, 1), :]
        qrows = jnp.concatenate(
            [jnp.where(seg_rows, jnp.broadcast_to(qrow[:, KV_DIM * g:KV_DIM * (g + 1)],
                                                  (N_KV_HEADS, KV_DIM)), 0.0)
             for g in range(GROUP)], axis=0)
        s_scr[pl.ds(pl.multiple_of(b * N_HEADS, N_HEADS), N_HEADS), :] = _dot_nt(
            qrows.astype(BF16), kb)
        return carry

    lax.fori_loop(0, chunk, scores, 0, unroll=SAMPLE_UNROLL)

    s = s_scr[...]
    sink = jnp.concatenate(
        [jnp.full((1, 1), sinks_ref[i] * LOG2E, F32) for i in range(N_HEADS)] * chunk, axis=0)
    m = jnp.maximum(jnp.max(s, axis=-1, keepdims=True), sink)
    p = jnp.exp2(s - m)
    den = jnp.sum(p, axis=-1, keepdims=True) + jnp.exp2(sink - m)
    p_scr[...] = (p * (1.0 / den)).astype(BF16)

    def outputs(b, carry):
        vb = jnp.where(expired, vn_ref[pl.ds(b, 1), :], cv_ref[b]).astype(BF16)
        r = _dot(p_scr[pl.ds(pl.multiple_of(b * N_HEADS, N_HEADS), N_HEADS), :], vb)
        o_scr[pl.ds(b, 1), :] = jnp.concatenate(
            [jnp.sum(jnp.where(seg_rows, r[N_KV_HEADS * g:N_KV_HEADS * (g + 1)], 0.0),
                     axis=0, keepdims=True) for g in range(GROUP)], axis=1)
        return carry

    lax.fori_loop(0, chunk, outputs, 0, unroll=SAMPLE_UNROLL)
    o_ref[...] = x + _dot(o_scr[...], wo_ref[...])


def _attn_sample(x, ck, cv, kn, vn, g, wq, gq, tables, sinks, wo, layer, chunk):
    m = x.shape[0]
    w_buf = ck.shape[1]
    row = pl.BlockSpec((chunk, D_MODEL), lambda i: (i, 0))
    tab = pl.BlockSpec((chunk, LANES), lambda i: (i, 0))
    new = pl.BlockSpec((chunk, KV_DIM), lambda i: (i, 0))
    cache = pl.BlockSpec((chunk, w_buf, KV_DIM), lambda i: (i, 0, 0))
    return pl.pallas_call(
        _attn_sample_body, grid=(m // chunk,),
        in_specs=[pl.BlockSpec(memory_space=pltpu.SMEM), row, cache, cache, new, new,
                  _const_spec((1, D_MODEL)), _layer_spec((D_MODEL, D_MODEL), layer),
                  _const_spec((1, D_MODEL)), tab, tab, tab, _layer_spec((D_MODEL, D_MODEL), layer)],
        out_specs=row,
        out_shape=jax.ShapeDtypeStruct(x.shape, F32),
        scratch_shapes=[pltpu.VMEM((chunk, D_MODEL), F32),
                        pltpu.VMEM((chunk * N_HEADS, w_buf), F32),
                        pltpu.VMEM((chunk * N_HEADS, w_buf), BF16),
                        pltpu.VMEM((chunk, D_MODEL), F32)],
        compiler_params=_params(1), name="attn_sample")(
            sinks, x, ck, cv, kn, vn, g, wq, gq, *tables, wo)


def _rope_tables(pos):
    half = ROT_DIM // 2
    inv_freq = ROPE_THETA ** (-jnp.arange(0, ROT_DIM, 2, dtype=F32) / ROT_DIM)
    ang = pos.astype(F32)[:, None] * inv_freq[None, :]
    cos, sin = jnp.cos(ang), jnp.sin(ang)
    n = pos.shape[0]
    rest = HEAD_DIM - ROT_DIM
    cos_h = jnp.concatenate([cos, cos, jnp.ones((n, rest), F32)], axis=1)
    sin_lo = jnp.concatenate([-sin, jnp.zeros((n, half + rest), F32)], axis=1)
    sin_hi = jnp.concatenate([jnp.zeros((n, half), F32), sin, jnp.zeros((n, rest), F32)], axis=1)
    reps = LANES // HEAD_DIM
    return tuple(jnp.tile(t, (1, reps)) for t in (cos_h, sin_lo, sin_hi))


def kernel(x_prompt, x_sample, state_conv, cache_k, cache_v, g_ffn1, w_ffn1_gate, w_ffn1_up,
           w_ffn1_down, g_mix, g_ffn2, w_ffn2_gate, w_ffn2_up, w_ffn2_down, w_in_a, conv_w,
           w_out_a, g_kv, w_kv, g_knorm, w_q, g_qnorm, sinks, w_o):
    nb, seq, _ = x_prompt.shape
    nd, dec_seq, _ = x_sample.shape
    assert dec_seq == 1 and cache_k.shape[1] == WINDOW
    depth = g_ffn1.shape[0]
    n_a = w_in_a.shape[0]
    past_len = seq
    w_buf = cache_k.shape[1]

    n_b = w_q.shape[0]
    ffn1 = (w_ffn1_gate, w_ffn1_up, w_ffn1_down)
    ffn2 = (w_ffn2_gate, w_ffn2_up, w_ffn2_down)
    w_in, w_out, wkv = w_in_a, w_out_a, w_kv
    wq = (w_q.reshape(n_b, D_MODEL, N_KV_HEADS, GROUP, HEAD_DIM).transpose(0, 1, 3, 2, 4)
          .reshape(n_b, D_MODEL, D_MODEL))
    wo = (w_o.reshape(n_b, N_KV_HEADS, GROUP, HEAD_DIM, D_MODEL).transpose(0, 2, 1, 3, 4)
          .reshape(n_b, D_MODEL, D_MODEL))
    sinks_gm = sinks.reshape(n_b, N_KV_HEADS, GROUP).transpose(0, 2, 1).reshape(n_b, N_HEADS)
    tab_p = _rope_tables(jnp.arange(seq, dtype=jnp.int32))
    tab_s = _rope_tables(jnp.full((nd,), past_len, jnp.int32))
    gk = jnp.tile(g_knorm, N_KV_HEADS)[None]

    xp = x_prompt
    xs = x_sample.reshape(nd, D_MODEL)
    ck = cache_k.reshape(nd, w_buf, KV_DIM)
    cv = cache_v.reshape(nd, w_buf, KV_DIM)
    conv_p, conv_s = [], []
    kp = vp = vtp = kn = vn = new_ck = new_cv = None

    def ffn_both(xp, xs, g, weights, layer):
        xp, xs = _ffn(xp.reshape(nb * seq, D_MODEL), xs, g[None], *weights, layer, TM_FFN)
        return xp.reshape(nb, seq, D_MODEL), xs

    for i in range(depth):
        if i == n_a:
            kp, vp, vtp = _kv_prompt(xp, g_kv[None], wkv, gk, tab_p, TM_KV)
            kn, vn, new_ck, new_cv = _kv_sample(xs, g_kv[None], wkv, gk, tab_s, ck, cv,
                                                SAMPLE_CHUNK)
        xp, xs = ffn_both(xp, xs, g_ffn1[i], ffn1, i)
        gm = g_mix[i][None]
        if i < n_a:
            xp, st = _conv_prompt(xp, jnp.zeros((nb, CONV_W - 1, D_MODEL), F32), gm, w_in,
                                  conv_w[i], w_out, i, TM_CONV)
            conv_p.append(st)
            xs, st = _conv_sample(xs, state_conv[i].reshape(nd, (CONV_W - 1) * D_MODEL), gm,
                                  w_in, conv_w[i], w_out, i)
            conv_s.append(st.reshape(nd, CONV_W - 1, D_MODEL))
        else:
            j = i - n_a
            gq = jnp.tile(g_qnorm[j], N_HEADS)[None]
            xp = _attn_prompt(xp, kp, vtp, gm, wq, gq, tab_p, sinks_gm[j], wo, j, TM_ATTN)
            xs = _attn_sample(xs, ck, cv, kn, vn, gm, wq, gq, tab_s, sinks_gm[j], wo, j,
                              SAMPLE_CHUNK)
        xp, xs = ffn_both(xp, xs, g_ffn2[i], ffn2, i)

    kv_shape = (w_buf, N_KV_HEADS, HEAD_DIM)
    return (xp, xs.reshape(nd, 1, D_MODEL), jnp.stack(conv_p),
            kp[:, seq - w_buf:].reshape(nb, *kv_shape), vp[:, seq - w_buf:].reshape(nb, *kv_shape),
            jnp.stack(conv_s), new_ck.reshape(nd, *kv_shape), new_cv.reshape(nd, *kv_shape))
```

```python
import functools
import math

import jax
import jax.numpy as jnp
from jax import lax
from jax.experimental import pallas as pl
from jax.experimental.pallas import tpu as pltpu

D_MODEL = 1024
D_FF = 2816
N_HEADS = 16
N_KV_HEADS = 4
GROUP = N_HEADS // N_KV_HEADS
HEAD_DIM = 64
KV_DIM = N_KV_HEADS * HEAD_DIM
ROT_DIM = HEAD_DIM // 4
ROPE_THETA = 500000.0
WINDOW = 128
EPS = 1e-6
LOG2E = math.log2(math.e)
CONV_W = 3

LANES = 128
SUBLANES = 8
VMEM_LIMIT = 56 * 1024 * 1024

FFN_CHUNK = 256
TM_FFN = 512
TM_CONV = 512
TM_KV = 1024
TM_ATTN = 1024
SAMPLE_CHUNK = 32
SAMPLE_UNROLL = 4

F32 = jnp.float32
BF16 = jnp.bfloat16


def _const_spec(shape):
    return pl.BlockSpec(shape, lambda *_: (0,) * len(shape), pipeline_mode=pl.Buffered(1))


def _layer_spec(shape, layer):
    return pl.BlockSpec((None,) + tuple(shape), lambda *_: (layer,) + (0,) * len(shape),
                        pipeline_mode=pl.Buffered(1))


def _params(n_axes):
    return pltpu.CompilerParams(dimension_semantics=("arbitrary",) * n_axes,
                                vmem_limit_bytes=VMEM_LIMIT)


def _dot(a, b):
    return jnp.dot(a, b, preferred_element_type=F32)


def _dot_nt(a, b):
    return lax.dot_general(a, b, (((1,), (1,)), ((), ())), preferred_element_type=F32)


def _rms(x, g):
    ms = jnp.mean(x * x, axis=-1, keepdims=True)
    return x * lax.rsqrt(ms + EPS) * g


def _head_mean_matrix():
    r = lax.broadcasted_iota(jnp.int32, (LANES, LANES), 0) // HEAD_DIM
    c = lax.broadcasted_iota(jnp.int32, (LANES, LANES), 1) // HEAD_DIM
    return jnp.where(r == c, 1.0 / HEAD_DIM, 0.0).astype(F32)


def _head_norm_rope(x, gain, cos, sin_lo, sin_hi):
    bd = _head_mean_matrix()
    outs = []
    for j in range(x.shape[1] // LANES):
        xs = x[:, LANES * j:LANES * (j + 1)]
        ms = _dot(xs * xs, bd)
        xn = xs * lax.rsqrt(ms + EPS) * gain[:, LANES * j:LANES * (j + 1)]
        outs.append(xn * cos + pltpu.roll(xn, ROT_DIM // 2, 1) * sin_hi
                    + pltpu.roll(xn, LANES - ROT_DIM // 2, 1) * sin_lo)
    return jnp.concatenate(outs, axis=1)


def _ffn_rows(x, g, wg_ref, wu_ref, wd_ref, before_chunk=None):
    h = _rms(x, g)
    acc = jnp.zeros_like(x)
    for c in range(D_FF // FFN_CHUNK):
        if before_chunk is not None:
            before_chunk(c)
        sl = slice(c * FFN_CHUNK, (c + 1) * FFN_CHUNK)
        gate = _dot(h, wg_ref[:, sl])
        up = _dot(h, wu_ref[:, sl])
        a = gate * jax.nn.sigmoid(gate) * up
        acc = acc + _dot(a, wd_ref[sl, :])
    return x + 0.5 * acc


def _ffn_body(xp_ref, xs_ref, g_ref, wg_hbm, wu_hbm, wd_hbm, op_ref, os_ref,
              wg_ref, wu_ref, wd_ref, sem, *, layer):
    step = pl.program_id(0)
    last = pl.num_programs(0) - 1
    n_chunks = D_FF // FFN_CHUNK

    def in_copies():
        blocks = [pl.ds(r, FFN_CHUNK) for r in range(0, D_MODEL, FFN_CHUNK)]
        return [pltpu.make_async_copy(hbm.at[layer, rows, :], ref.at[rows, :], sem.at[j, k])
                for k, rows in enumerate(blocks)
                for j, (hbm, ref) in enumerate(((wg_hbm, wg_ref), (wu_hbm, wu_ref)))]

    def down_copy(c):
        rows = pl.ds(c * FFN_CHUNK, FFN_CHUNK)
        return pltpu.make_async_copy(wd_hbm.at[layer, rows, :], wd_ref.at[rows, :], sem.at[2, c])

    def wait_chunk(c):
        if c == 0:
            for copy in in_copies():
                copy.wait()
        down_copy(c).wait()

    @pl.when(step == 0)
    def _():
        for copy in in_copies():
            copy.start()
        for c in range(n_chunks):
            down_copy(c).start()
        op_ref[...] = _ffn_rows(xp_ref[...], g_ref[...], wg_ref, wu_ref, wd_ref, wait_chunk)

    @pl.when((step > 0) & (step < last))
    def _():
        op_ref[...] = _ffn_rows(xp_ref[...], g_ref[...], wg_ref, wu_ref, wd_ref)

    @pl.when(step == last)
    def _():
        os_ref[...] = _ffn_rows(xs_ref[...], g_ref[...], wg_ref, wu_ref, wd_ref)


def _ffn(xp, xs, g, wg, wu, wd, layer, tm):
    m, ms = xp.shape[0], xs.shape[0]
    n_p = m // tm
    assert n_p >= 1 and D_FF % FFN_CHUNK == 0
    row_p = pl.BlockSpec((tm, D_MODEL), lambda i: (jnp.minimum(i, n_p - 1), 0))
    hbm = pl.BlockSpec(memory_space=pl.ANY)
    return pl.pallas_call(
        functools.partial(_ffn_body, layer=layer), grid=(n_p + 1,),
        in_specs=[row_p, _const_spec((ms, D_MODEL)), _const_spec((1, D_MODEL)), hbm, hbm, hbm],
        out_specs=[row_p, pl.BlockSpec((ms, D_MODEL), lambda i: (0, 0))],
        out_shape=[jax.ShapeDtypeStruct((m, D_MODEL), F32),
                   jax.ShapeDtypeStruct((ms, D_MODEL), F32)],
        scratch_shapes=[pltpu.VMEM((D_MODEL, D_FF), F32), pltpu.VMEM((D_MODEL, D_FF), F32),
                        pltpu.VMEM((D_FF, D_MODEL), F32),
                        pltpu.SemaphoreType.DMA((3, D_FF // FFN_CHUNK))],
        compiler_params=_params(1), name="ffn")(xp, xs, g, wg, wu, wd)


def _conv_prompt_body(x_ref, st_ref, g_ref, win_ref, cw_ref, wout_ref, o_ref, nst_ref, ext_ref):
    tm = x_ref.shape[0]
    lead = SUBLANES - (CONV_W - 1)

    @pl.when(pl.program_id(1) == 0)
    def _():
        ext_ref[lead:SUBLANES, :] = st_ref[...]

    x = x_ref[...]
    bcu = _dot(_rms(x, g_ref[...]), win_ref[...])
    b = bcu[:, :D_MODEL]
    cu = bcu[:, D_MODEL:2 * D_MODEL] * bcu[:, 2 * D_MODEL:]
    ext_ref[SUBLANES:SUBLANES + tm, :] = cu
    cw = cw_ref[...]
    conv = (cw[0:1] * ext_ref[lead:lead + tm, :] + cw[1:2] * ext_ref[lead + 1:lead + 1 + tm, :]
            + cw[2:3] * cu)
    y = _dot(b * conv, wout_ref[...])
    o_ref[...] = x + y
    tail = ext_ref[lead + tm:SUBLANES + tm, :]
    ext_ref[lead:SUBLANES, :] = tail
    nst_ref[...] = tail


def _conv_prompt(x, state, g, w_in, cw, w_out, layer, tm):
    nb, s, _ = x.shape
    row = pl.BlockSpec((None, tm, D_MODEL), lambda b, t: (b, t, 0))
    st = pl.BlockSpec((None, CONV_W - 1, D_MODEL), lambda b, t: (b, 0, 0))
    return pl.pallas_call(
        _conv_prompt_body, grid=(nb, s // tm),
        in_specs=[row, st, _const_spec((1, D_MODEL)), _layer_spec((D_MODEL, 3 * D_MODEL), layer),
                  _const_spec((CONV_W, D_MODEL)), _layer_spec((D_MODEL, D_MODEL), layer)],
        out_specs=[row, st],
        out_shape=[jax.ShapeDtypeStruct(x.shape, F32),
                   jax.ShapeDtypeStruct((nb, CONV_W - 1, D_MODEL), F32)],
        scratch_shapes=[pltpu.VMEM((tm + SUBLANES, D_MODEL), F32)],
        compiler_params=_params(2), name="conv_prompt")(x, state, g, w_in, cw, w_out)


def _conv_sample_body(x_ref, st_ref, g_ref, win_ref, cw_ref, wout_ref, o_ref, nst_ref):
    x = x_ref[...]
    bcu = _dot(_rms(x, g_ref[...]), win_ref[...])
    b = bcu[:, :D_MODEL]
    cu = bcu[:, D_MODEL:2 * D_MODEL] * bcu[:, 2 * D_MODEL:]
    cw = cw_ref[...]
    s1 = st_ref[:, D_MODEL:]
    conv = cw[0:1] * st_ref[:, :D_MODEL] + cw[1:2] * s1 + cw[2:3] * cu
    o_ref[...] = x + _dot(b * conv, wout_ref[...])
    nst_ref[:, :D_MODEL] = s1
    nst_ref[:, D_MODEL:] = cu


def _conv_sample(x, state, g, w_in, cw, w_out, layer):
    m = x.shape[0]
    return pl.pallas_call(
        _conv_sample_body, grid=(1,),
        in_specs=[_const_spec((m, D_MODEL)), _const_spec((m, 2 * D_MODEL)),
                  _const_spec((1, D_MODEL)), _layer_spec((D_MODEL, 3 * D_MODEL), layer),
                  _const_spec((CONV_W, D_MODEL)), _layer_spec((D_MODEL, D_MODEL), layer)],
        out_specs=[pl.BlockSpec((m, D_MODEL), lambda i: (0, 0)),
                   pl.BlockSpec((m, 2 * D_MODEL), lambda i: (0, 0))],
        out_shape=[jax.ShapeDtypeStruct((m, D_MODEL), F32),
                   jax.ShapeDtypeStruct((m, 2 * D_MODEL), F32)],
        compiler_params=_params(1), name="conv_sample")(x, state, g, w_in, cw, w_out)


def _kv_rows(x, g, w, gk, cos, sin_lo, sin_hi):
    kv = _dot(_rms(x, g), w)
    k = _head_norm_rope(kv[:, :KV_DIM], gk, cos, sin_lo, sin_hi)
    return k, kv[:, KV_DIM:]


def _kv_prompt_body(x_ref, g_ref, w_ref, gk_ref, cos_ref, slo_ref, shi_ref, k_ref, v_ref, vt_ref):
    k, v = _kv_rows(x_ref[...], g_ref[...], w_ref[...], gk_ref[...],
                    cos_ref[...], slo_ref[...], shi_ref[...])
    k_ref[...] = k
    v_ref[...] = v
    vt_ref[...] = v.T


def _kv_prompt(x, g, w, gk, tables, tm):
    nb, s, _ = x.shape
    row = pl.BlockSpec((None, tm, D_MODEL), lambda b, t: (b, t, 0))
    tab = pl.BlockSpec((tm, LANES), lambda b, t: (t, 0))
    out = pl.BlockSpec((None, tm, KV_DIM), lambda b, t: (b, t, 0))
    out_t = pl.BlockSpec((None, KV_DIM, tm), lambda b, t: (b, 0, t))
    return pl.pallas_call(
        _kv_prompt_body, grid=(nb, s // tm),
        in_specs=[row, _const_spec((1, D_MODEL)), _const_spec((D_MODEL, 2 * KV_DIM)),
                  _const_spec((1, KV_DIM)), tab, tab, tab],
        out_specs=[out, out, out_t],
        out_shape=[jax.ShapeDtypeStruct((nb, s, KV_DIM), F32)] * 2
        + [jax.ShapeDtypeStruct((nb, KV_DIM, s), F32)],
        compiler_params=_params(2), name="kv_prompt")(x, g, w, gk, *tables)


def _kv_sample_body(x_ref, g_ref, w_ref, gk_ref, cos_ref, slo_ref, shi_ref, ck_ref, cv_ref,
                    k_ref, v_ref, nk_ref, nv_ref):
    k, v = _kv_rows(x_ref[...], g_ref[...], w_ref[...], gk_ref[...],
                    cos_ref[...], slo_ref[...], shi_ref[...])
    k_ref[...] = k
    v_ref[...] = v
    w_buf = ck_ref.shape[1]

    def shift(b, carry):
        nk_ref[b, 0:w_buf - 1, :] = ck_ref[b, 1:w_buf, :]
        nv_ref[b, 0:w_buf - 1, :] = cv_ref[b, 1:w_buf, :]
        nk_ref[b, w_buf - 1:w_buf, :] = k_ref[pl.ds(b, 1), :]
        nv_ref[b, w_buf - 1:w_buf, :] = v_ref[pl.ds(b, 1), :]
        return carry

    lax.fori_loop(0, x_ref.shape[0], shift, 0)


def _kv_sample(x, g, w, gk, tables, ck, cv, chunk):
    m = x.shape[0]
    w_buf = ck.shape[1]
    row = pl.BlockSpec((chunk, D_MODEL), lambda i: (i, 0))
    tab = pl.BlockSpec((chunk, LANES), lambda i: (i, 0))
    new = pl.BlockSpec((chunk, KV_DIM), lambda i: (i, 0))
    cache = pl.BlockSpec((chunk, w_buf, KV_DIM), lambda i: (i, 0, 0))
    return pl.pallas_call(
        _kv_sample_body, grid=(m // chunk,),
        in_specs=[row, _const_spec((1, D_MODEL)), _const_spec((D_MODEL, 2 * KV_DIM)),
                  _const_spec((1, KV_DIM)), tab, tab, tab, cache, cache],
        out_specs=[new, new, cache, cache],
        out_shape=[jax.ShapeDtypeStruct((m, KV_DIM), F32)] * 2
        + [jax.ShapeDtypeStruct(ck.shape, F32)] * 2,
        compiler_params=_params(1), name="kv_sample")(x, g, w, gk, *tables, ck, cv)


def _queries(x, g, wq, gq, cos, sin_lo, sin_hi):
    q = _dot(_rms(x, g), wq)
    return _head_norm_rope(q, gq * (LOG2E / math.sqrt(HEAD_DIM)), cos, sin_lo, sin_hi)


def _attn_prompt_body(sinks_ref, x_ref, kc_ref, kp_ref, vtc_ref, vtp_ref, g_ref, wq_ref, gq_ref,
                      cos_ref, slo_ref, shi_ref, wo_ref, o_ref, q_scr, o_scr):
    tm = x_ref.shape[0]
    first_tile = pl.program_id(1) == 0
    x = x_ref[...]
    q_scr[...] = _queries(x, g_ref[...], wq_ref[...], gq_ref[...],
                          cos_ref[...], slo_ref[...], shi_ref[...]).astype(BF16)
    kcat = jnp.concatenate([kp_ref[...], kc_ref[...]], axis=0).astype(BF16)
    vtcat = jnp.concatenate([vtp_ref[...], vtc_ref[...]], axis=1).astype(BF16)

    seg = lax.broadcasted_iota(jnp.int32, (1, KV_DIM), 1) // HEAD_DIM
    kj = lax.broadcasted_iota(jnp.int32, (2 * WINDOW, WINDOW), 0)
    qi = lax.broadcasted_iota(jnp.int32, (2 * WINDOW, WINDOW), 1)
    rel = qi + WINDOW - kj
    band = (rel >= 0) & (rel < WINDOW)
    band0 = band & ((kj >= WINDOW) | jnp.logical_not(first_tile))
    bias = jnp.where(band, 0.0, -jnp.inf).astype(F32)
    bias0 = jnp.where(band0, 0.0, -jnp.inf).astype(F32)
    no_rows = jnp.zeros((HEAD_DIM, 2 * WINDOW), BF16)

    for i in range(tm // WINDOW):
        rows = slice(WINDOW * i, WINDOW * (i + 1))
        kw = kcat[WINDOW * i:WINDOW * (i + 2)]
        vtw = vtcat[:, WINDOW * i:WINDOW * (i + 2)]
        q4 = jnp.concatenate(
            [q_scr[rows, KV_DIM * g:KV_DIM * (g + 1)] for g in range(GROUP)], axis=0)
        bias4 = jnp.concatenate([bias0 if i == 0 else bias] * GROUP, axis=1)
        probs, vts, rdens = [], [], []
        for h in range(N_KV_HEADS):
            s = _dot_nt(jnp.where(seg == h, kw, jnp.zeros_like(kw)), q4) + bias4
            sink = jnp.concatenate(
                [jnp.full((1, WINDOW), sinks_ref[N_KV_HEADS * g + h] * LOG2E, F32)
                 for g in range(GROUP)], axis=1)
            m = jnp.maximum(jnp.max(s, axis=0, keepdims=True), sink)
            p = jnp.exp2(s - m)
            rdens.append(1.0 / (jnp.sum(p, axis=0, keepdims=True) + jnp.exp2(sink - m)))
            probs.append(p.astype(BF16))
            vts.append(jnp.concatenate(
                [vtw[HEAD_DIM * h:HEAD_DIM * (h + 1)] if hh == h else no_rows
                 for hh in range(N_KV_HEADS)], axis=0))
        o_t = _dot(jnp.concatenate(vts, axis=1), jnp.concatenate(probs, axis=0))
        o_t = jnp.concatenate(
            [o_t[HEAD_DIM * h:HEAD_DIM * (h + 1)] * rdens[h] for h in range(N_KV_HEADS)], axis=0)
        for g in range(GROUP):
            o_scr[rows, KV_DIM * g:KV_DIM * (g + 1)] = (
                o_t[:, WINDOW * g:WINDOW * (g + 1)].T)

    o_ref[...] = x + _dot(o_scr[...], wo_ref[...])


def _attn_prompt(x, k, vt, g, wq, gq, tables, sinks, wo, layer, tm):
    nb, s, _ = x.shape
    per_tile = tm // WINDOW
    row = pl.BlockSpec((None, tm, D_MODEL), lambda b, t: (b, t, 0))
    cur = pl.BlockSpec((None, tm, KV_DIM), lambda b, t: (b, t, 0))
    prev = pl.BlockSpec((None, WINDOW, KV_DIM),
                        lambda b, t: (b, jnp.maximum(t * per_tile - 1, 0), 0))
    cur_t = pl.BlockSpec((None, KV_DIM, tm), lambda b, t: (b, 0, t))
    prev_t = pl.BlockSpec((None, KV_DIM, WINDOW),
                          lambda b, t: (b, 0, jnp.maximum(t * per_tile - 1, 0)))
    tab = pl.BlockSpec((tm, LANES), lambda b, t: (t, 0))
    return pl.pallas_call(
        _attn_prompt_body, grid=(nb, s // tm),
        in_specs=[pl.BlockSpec(memory_space=pltpu.SMEM), row, cur, prev, cur_t, prev_t,
                  _const_spec((1, D_MODEL)), _layer_spec((D_MODEL, D_MODEL), layer),
                  _const_spec((1, D_MODEL)), tab, tab, tab, _layer_spec((D_MODEL, D_MODEL), layer)],
        out_specs=row,
        out_shape=jax.ShapeDtypeStruct(x.shape, F32),
        scratch_shapes=[pltpu.VMEM((tm, D_MODEL), BF16), pltpu.VMEM((tm, D_MODEL), F32)],
        compiler_params=_params(2), name="attn_prompt")(
            sinks, x, k, k, vt, vt, g, wq, gq, *tables, wo)


def _attn_sample_body(sinks_ref, x_ref, ck_ref, cv_ref, kn_ref, vn_ref, g_ref, wq_ref, gq_ref,
                      cos_ref, slo_ref, shi_ref, wo_ref, o_ref, q_scr, s_scr, p_scr, o_scr):
    chunk = x_ref.shape[0]
    x = x_ref[...]
    q_scr[...] = _queries(x, g_ref[...], wq_ref[...], gq_ref[...],
                          cos_ref[...], slo_ref[...], shi_ref[...])
    w_buf = ck_ref.shape[1]
    seg_rows = (lax.broadcasted_iota(jnp.int32, (N_KV_HEADS, KV_DIM), 1) // HEAD_DIM
                == lax.broadcasted_iota(jnp.int32, (N_KV_HEADS, KV_DIM), 0))
    expired = lax.broadcasted_iota(jnp.int32, (w_buf, 1), 0) == 0

    def scores(b, carry):
        kb = jnp.where(expired, kn_ref[pl.ds(b, 1), :], ck_ref[b]).astype(BF16)
        qrow = q_scr[pl.ds(b, 1), :]
        qrows = jnp.concatenate(
            [jnp.where(seg_rows, jnp.broadcast_to(qrow[:, KV_DIM * g:KV_DIM * (g + 1)],
                                                  (N_KV_HEADS, KV_DIM)), 0.0)
             for g in range(GROUP)], axis=0)
        s_scr[pl.ds(pl.multiple_of(b * N_HEADS, N_HEADS), N_HEADS), :] = _dot_nt(
            qrows.astype(BF16), kb)
        return carry

    lax.fori_loop(0, chunk, scores, 0, unroll=SAMPLE_UNROLL)

    s = s_scr[...]
    sink = jnp.concatenate(
        [jnp.full((1, 1), sinks_ref[i] * LOG2E, F32) for i in range(N_HEADS)] * chunk, axis=0)
    m = jnp.maximum(jnp.max(s, axis=-1, keepdims=True), sink)
    p = jnp.exp2(s - m)
    den = jnp.sum(p, axis=-1, keepdims=True) + jnp.exp2(sink - m)
    p_scr[...] = (p * (1.0 / den)).astype(BF16)

    def outputs(b, carry):
        vb = jnp.where(expired, vn_ref[pl.ds(b, 1), :], cv_ref[b]).astype(BF16)
        r = _dot(p_scr[pl.ds(pl.multiple_of(b * N_HEADS, N_HEADS), N_HEADS), :], vb)
        o_scr[pl.ds(b, 1), :] = jnp.concatenate(
            [jnp.sum(jnp.where(seg_rows, r[N_KV_HEADS * g:N_KV_HEADS * (g + 1)], 0.0),
                     axis=0, keepdims=True) for g in range(GROUP)], axis=1)
        return carry

    lax.fori_loop(0, chunk, outputs, 0, unroll=SAMPLE_UNROLL)
    o_ref[...] = x + _dot(o_scr[...], wo_ref[...])


def _attn_sample(x, ck, cv, kn, vn, g, wq, gq, tables, sinks, wo, layer, chunk):
    m = x.shape[0]
    w_buf = ck.shape[1]
    row = pl.BlockSpec((chunk, D_MODEL), lambda i: (i, 0))
    tab = pl.BlockSpec((chunk, LANES), lambda i: (i, 0))
    new = pl.BlockSpec((chunk, KV_DIM), lambda i: (i, 0))
    cache = pl.BlockSpec((chunk, w_buf, KV_DIM), lambda i: (i, 0, 0))
    return pl.pallas_call(
        _attn_sample_body, grid=(m // chunk,),
        in_specs=[pl.BlockSpec(memory_space=pltpu.SMEM), row, cache, cache, new, new,
                  _const_spec((1, D_MODEL)), _layer_spec((D_MODEL, D_MODEL), layer),
                  _const_spec((1, D_MODEL)), tab, tab, tab, _layer_spec((D_MODEL, D_MODEL), layer)],
        out_specs=row,
        out_shape=jax.ShapeDtypeStruct(x.shape, F32),
        scratch_shapes=[pltpu.VMEM((chunk, D_MODEL), F32),
                        pltpu.VMEM((chunk * N_HEADS, w_buf), F32),
                        pltpu.VMEM((chunk * N_HEADS, w_buf), BF16),
                        pltpu.VMEM((chunk, D_MODEL), F32)],
        compiler_params=_params(1), name="attn_sample")(
            sinks, x, ck, cv, kn, vn, g, wq, gq, *tables, wo)


def _rope_tables(pos):
    half = ROT_DIM // 2
    inv_freq = ROPE_THETA ** (-jnp.arange(0, ROT_DIM, 2, dtype=F32) / ROT_DIM)
    ang = pos.astype(F32)[:, None] * inv_freq[None, :]
    cos, sin = jnp.cos(ang), jnp.sin(ang)
    n = pos.shape[0]
    rest = HEAD_DIM - ROT_DIM
    cos_h = jnp.concatenate([cos, cos, jnp.ones((n, rest), F32)], axis=1)
    sin_lo = jnp.concatenate([-sin, jnp.zeros((n, half + rest), F32)], axis=1)
    sin_hi = jnp.concatenate([jnp.zeros((n, half), F32), sin, jnp.zeros((n, rest), F32)], axis=1)
    reps = LANES // HEAD_DIM
    return tuple(jnp.tile(t, (1, reps)) for t in (cos_h, sin_lo, sin_hi))


def kernel(x_prompt, x_sample, state_conv, cache_k, cache_v, g_ffn1, w_ffn1_gate, w_ffn1_up,
           w_ffn1_down, g_mix, g_ffn2, w_ffn2_gate, w_ffn2_up, w_ffn2_down, w_in_a, conv_w,
           w_out_a, g_kv, w_kv, g_knorm, w_q, g_qnorm, sinks, w_o):
    nb, seq, _ = x_prompt.shape
    nd, dec_seq, _ = x_sample.shape
    assert dec_seq == 1 and cache_k.shape[1] == WINDOW
    depth = g_ffn1.shape[0]
    n_a = w_in_a.shape[0]
    past_len = seq
    w_buf = cache_k.shape[1]

    n_b = w_q.shape[0]
    ffn1 = (w_ffn1_gate, w_ffn1_up, w_ffn1_down)
    ffn2 = (w_ffn2_gate, w_ffn2_up, w_ffn2_down)
    w_in, w_out, wkv = w_in_a, w_out_a, w_kv
    wq = (w_q.reshape(n_b, D_MODEL, N_KV_HEADS, GROUP, HEAD_DIM).transpose(0, 1, 3, 2, 4)
          .reshape(n_b, D_MODEL, D_MODEL))
    wo = (w_o.reshape(n_b, N_KV_HEADS, GROUP, HEAD_DIM, D_MODEL).transpose(0, 2, 1, 3, 4)
          .reshape(n_b, D_MODEL, D_MODEL))
    sinks_gm = sinks.reshape(n_b, N_KV_HEADS, GROUP).transpose(0, 2, 1).reshape(n_b, N_HEADS)
    tab_p = _rope_tables(jnp.arange(seq, dtype=jnp.int32))
    tab_s = _rope_tables(jnp.full((nd,), past_len, jnp.int32))
    gk = jnp.tile(g_knorm, N_KV_HEADS)[None]

    xp = x_prompt
    xs = x_sample.reshape(nd, D_MODEL)
    ck = cache_k.reshape(nd, w_buf, KV_DIM)
    cv = cache_v.reshape(nd, w_buf, KV_DIM)
    conv_p, conv_s = [], []
    kp = vp = vtp = kn = vn = new_ck = new_cv = None

    def ffn_both(xp, xs, g, weights, layer):
        xp, xs = _ffn(xp.reshape(nb * seq, D_MODEL), xs, g[None], *weights, layer, TM_FFN)
        return xp.reshape(nb, seq, D_MODEL), xs

    for i in range(depth):
        if i == n_a:
            kp, vp, vtp = _kv_prompt(xp, g_kv[None], wkv, gk, tab_p, TM_KV)
            kn, vn, new_ck, new_cv = _kv_sample(xs, g_kv[None], wkv, gk, tab_s, ck, cv,
                                                SAMPLE_CHUNK)
        xp, xs = ffn_both(xp, xs, g_ffn1[i], ffn1, i)
        gm = g_mix[i][None]
        if i < n_a:
            xp, st = _conv_prompt(xp, jnp.zeros((nb, CONV_W - 1, D_MODEL), F32), gm, w_in,
                                  conv_w[i], w_out, i, TM_CONV)
            conv_p.append(st)
            xs, st = _conv_sample(xs, state_conv[i].reshape(nd, (CONV_W - 1) * D_MODEL), gm,
                                  w_in, conv_w[i], w_out, i)
            conv_s.append(st.reshape(nd, CONV_W - 1, D_MODEL))
        else:
            j = i - n_a
            gq = jnp.tile(g_qnorm[j], N_HEADS)[None]
            xp = _attn_prompt(xp, kp, vtp, gm, wq, gq, tab_p, sinks_gm[j], wo, j, TM_ATTN)
            xs = _attn_sample(xs, ck, cv, kn, vn, gm, wq, gq, tab_s, sinks_gm[j], wo, j,
                              SAMPLE_CHUNK)
        xp, xs = ffn_both(xp, xs, g_ffn2[i], ffn2, i)

    kv_shape = (w_buf, N_KV_HEADS, HEAD_DIM)
    return (xp, xs.reshape(nd, 1, D_MODEL), jnp.stack(conv_p),
            kp[:, seq - w_buf:].reshape(nb, *kv_shape), vp[:, seq - w_buf:].reshape(nb, *kv_shape),
            jnp.stack(conv_s), new_ck.reshape(nd, *kv_shape), new_cv.reshape(nd, *kv_shape))
```

```python
import functools
import math

import jax
import jax.numpy as jnp
from jax import lax
from jax.experimental import pallas as pl
from jax.experimental.pallas import tpu as pltpu

D_MODEL = 1024
D_FF = 2816
N_HEADS = 16
N_KV_HEADS = 4
GROUP = N_HEADS // N_KV_HEADS
HEAD_DIM = 64
KV_DIM = N_KV_HEADS * HEAD_DIM
ROT_DIM = HEAD_DIM // 4
ROPE_THETA = 500000.0
WINDOW = 128
EPS = 1e-6
LOG2E = math.log2(math.e)
CONV_W = 3

LANES = 128
SUBLANES = 8
VMEM_LIMIT = 56 * 1024 * 1024

FFN_CHUNK = 256
TM_FFN = 512
TM_CONV = 1024
TM_KV = 1024
TM_ATTN = 1024
SAMPLE_CHUNK = 32
SAMPLE_UNROLL = 4

F32 = jnp.float32
BF16 = jnp.bfloat16


def _const_spec(shape):
    return pl.BlockSpec(shape, lambda *_: (0,) * len(shape), pipeline_mode=pl.Buffered(1))


def _layer_spec(shape, layer):
    return pl.BlockSpec((None,) + tuple(shape), lambda *_: (layer,) + (0,) * len(shape),
                        pipeline_mode=pl.Buffered(1))


def _params(n_axes):
    return pltpu.CompilerParams(dimension_semantics=("arbitrary",) * n_axes,
                                vmem_limit_bytes=VMEM_LIMIT)


def _dot(a, b):
    return jnp.dot(a, b, preferred_element_type=F32)


def _dot_nt(a, b):
    return lax.dot_general(a, b, (((1,), (1,)), ((), ())), preferred_element_type=F32)


def _rms(x, g):
    ms = jnp.mean(x * x, axis=-1, keepdims=True)
    return x * lax.rsqrt(ms + EPS) * g


def _head_mean_matrix():
    r = lax.broadcasted_iota(jnp.int32, (LANES, LANES), 0) // HEAD_DIM
    c = lax.broadcasted_iota(jnp.int32, (LANES, LANES), 1) // HEAD_DIM
    return jnp.where(r == c, 1.0 / HEAD_DIM, 0.0).astype(F32)


def _head_norm_rope(x, gain, cos, sin_lo, sin_hi):
    bd = _head_mean_matrix()
    outs = []
    for j in range(x.shape[1] // LANES):
        xs = x[:, LANES * j:LANES * (j + 1)]
        ms = _dot(xs * xs, bd)
        xn = xs * lax.rsqrt(ms + EPS) * gain[:, LANES * j:LANES * (j + 1)]
        outs.append(xn * cos + pltpu.roll(xn, ROT_DIM // 2, 1) * sin_hi
                    + pltpu.roll(xn, LANES - ROT_DIM // 2, 1) * sin_lo)
    return jnp.concatenate(outs, axis=1)


def _ffn_rows(x, g, wg_ref, wu_ref, wd_ref):
    h = _rms(x, g)
    acc = jnp.zeros_like(x)
    for c in range(D_FF // FFN_CHUNK):
        sl = slice(c * FFN_CHUNK, (c + 1) * FFN_CHUNK)
        gate = _dot(h, wg_ref[:, sl])
        up = _dot(h, wu_ref[:, sl])
        a = gate * jax.nn.sigmoid(gate) * up
        acc = acc + _dot(a, wd_ref[sl, :])
    return x + 0.5 * acc


def _ffn_body(xp_ref, xs_ref, g_ref, wg_ref, wu_ref, wd_ref, op_ref, os_ref):
    step = pl.program_id(0)
    last = pl.num_programs(0) - 1

    @pl.when(step < last)
    def _():
        op_ref[...] = _ffn_rows(xp_ref[...], g_ref[...], wg_ref, wu_ref, wd_ref)

    @pl.when(step == last)
    def _():
        os_ref[...] = _ffn_rows(xs_ref[...], g_ref[...], wg_ref, wu_ref, wd_ref)


def _ffn(xp, xs, g, wg, wu, wd, layer, tm):
    m, ms = xp.shape[0], xs.shape[0]
    n_p = m // tm
    row_p = pl.BlockSpec((tm, D_MODEL), lambda i: (jnp.minimum(i, n_p - 1), 0))
    return pl.pallas_call(
        _ffn_body, grid=(n_p + 1,),
        in_specs=[row_p, _const_spec((ms, D_MODEL)), _const_spec((1, D_MODEL)),
                  _layer_spec((D_MODEL, D_FF), layer), _layer_spec((D_MODEL, D_FF), layer),
                  _layer_spec((D_FF, D_MODEL), layer)],
        out_specs=[row_p, pl.BlockSpec((ms, D_MODEL), lambda i: (0, 0))],
        out_shape=[jax.ShapeDtypeStruct((m, D_MODEL), F32),
                   jax.ShapeDtypeStruct((ms, D_MODEL), F32)],
        compiler_params=_params(1), name="ffn")(xp, xs, g, wg, wu, wd)


def _conv_project(x, g, win_ref):
    bcu = _dot(_rms(x, g), win_ref[...])
    return bcu[:, :D_MODEL], bcu[:, D_MODEL:2 * D_MODEL] * bcu[:, 2 * D_MODEL:]


def _conv_body(xp_ref, stp_ref, xs_ref, sts_ref, g_ref, win_ref, cw_ref, wout_ref,
               op_ref, nstp_ref, os_ref, nsts_ref, ext_ref, *, tiles_per_seq):
    step = pl.program_id(0)
    last = pl.num_programs(0) - 1
    cw = cw_ref[...]

    @pl.when(step < last)
    def _():
        tm = xp_ref.shape[0]
        lead = SUBLANES - (CONV_W - 1)

        @pl.when(lax.rem(step, tiles_per_seq) == 0)
        def _():
            ext_ref[lead:SUBLANES, :] = stp_ref[...]

        x = xp_ref[...]
        b, cu = _conv_project(x, g_ref[...], win_ref)
        ext_ref[SUBLANES:SUBLANES + tm, :] = cu
        conv = (cw[0:1] * ext_ref[lead:lead + tm, :]
                + cw[1:2] * ext_ref[lead + 1:lead + 1 + tm, :] + cw[2:3] * cu)
        op_ref[...] = x + _dot(b * conv, wout_ref[...])
        tail = ext_ref[lead + tm:SUBLANES + tm, :]
        ext_ref[lead:SUBLANES, :] = tail
        nstp_ref[...] = tail

    @pl.when(step == last)
    def _():
        x = xs_ref[...]
        b, cu = _conv_project(x, g_ref[...], win_ref)
        s1 = sts_ref[:, D_MODEL:]
        conv = cw[0:1] * sts_ref[:, :D_MODEL] + cw[1:2] * s1 + cw[2:3] * cu
        os_ref[...] = x + _dot(b * conv, wout_ref[...])
        nsts_ref[:, :D_MODEL] = s1
        nsts_ref[:, D_MODEL:] = cu


def _conv(xp, state_p, xs, state_s, g, w_in, cw, w_out, layer, tm):
    nb, s, _ = xp.shape
    ms = xs.shape[0]
    nt = s // tm
    n_p = nb * nt

    def tile(i):
        return jnp.minimum(i, n_p - 1)

    row = pl.BlockSpec((None, tm, D_MODEL), lambda i: (tile(i) // nt, tile(i) % nt, 0))
    st = pl.BlockSpec((None, CONV_W - 1, D_MODEL), lambda i: (tile(i) // nt, 0, 0))
    return pl.pallas_call(
        functools.partial(_conv_body, tiles_per_seq=nt), grid=(n_p + 1,),
        in_specs=[row, st, _const_spec((ms, D_MODEL)), _const_spec((ms, 2 * D_MODEL)),
                  _const_spec((1, D_MODEL)), _layer_spec((D_MODEL, 3 * D_MODEL), layer),
                  _const_spec((CONV_W, D_MODEL)), _layer_spec((D_MODEL, D_MODEL), layer)],
        out_specs=[row, st, pl.BlockSpec((ms, D_MODEL), lambda i: (0, 0)),
                   pl.BlockSpec((ms, 2 * D_MODEL), lambda i: (0, 0))],
        out_shape=[jax.ShapeDtypeStruct(xp.shape, F32),
                   jax.ShapeDtypeStruct((nb, CONV_W - 1, D_MODEL), F32),
                   jax.ShapeDtypeStruct((ms, D_MODEL), F32),
                   jax.ShapeDtypeStruct((ms, 2 * D_MODEL), F32)],
        scratch_shapes=[pltpu.VMEM((tm + SUBLANES, D_MODEL), F32)],
        compiler_params=_params(1), name="conv")(xp, state_p, xs, state_s, g, w_in, cw, w_out)


def _kv_rows(x, g, w, gk, cos, sin_lo, sin_hi):
    kv = _dot(_rms(x, g), w)
    k = _head_norm_rope(kv[:, :KV_DIM], gk, cos, sin_lo, sin_hi)
    return k, kv[:, KV_DIM:]


def _kv_prompt_body(x_ref, g_ref, w_ref, gk_ref, cos_ref, slo_ref, shi_ref, k_ref, v_ref, vt_ref):
    k, v = _kv_rows(x_ref[...], g_ref[...], w_ref[...], gk_ref[...],
                    cos_ref[...], slo_ref[...], shi_ref[...])
    k_ref[...] = k
    v_ref[...] = v
    vt_ref[...] = v.T


def _kv_prompt(x, g, w, gk, tables, tm):
    nb, s, _ = x.shape
    row = pl.BlockSpec((None, tm, D_MODEL), lambda b, t: (b, t, 0))
    tab = pl.BlockSpec((tm, LANES), lambda b, t: (t, 0))
    out = pl.BlockSpec((None, tm, KV_DIM), lambda b, t: (b, t, 0))
    out_t = pl.BlockSpec((None, KV_DIM, tm), lambda b, t: (b, 0, t))
    return pl.pallas_call(
        _kv_prompt_body, grid=(nb, s // tm),
        in_specs=[row, _const_spec((1, D_MODEL)), _const_spec((D_MODEL, 2 * KV_DIM)),
                  _const_spec((1, KV_DIM)), tab, tab, tab],
        out_specs=[out, out, out_t],
        out_shape=[jax.ShapeDtypeStruct((nb, s, KV_DIM), F32)] * 2
        + [jax.ShapeDtypeStruct((nb, KV_DIM, s), F32)],
        compiler_params=_params(2), name="kv_prompt")(x, g, w, gk, *tables)


def _kv_sample_body(x_ref, g_ref, w_ref, gk_ref, cos_ref, slo_ref, shi_ref, ck_ref, cv_ref,
                    k_ref, v_ref, nk_ref, nv_ref):
    k, v = _kv_rows(x_ref[...], g_ref[...], w_ref[...], gk_ref[...],
                    cos_ref[...], slo_ref[...], shi_ref[...])
    k_ref[...] = k
    v_ref[...] = v
    w_buf = ck_ref.shape[1]

    def shift(b, carry):
        nk_ref[b, 0:w_buf - 1, :] = ck_ref[b, 1:w_buf, :]
        nv_ref[b, 0:w_buf - 1, :] = cv_ref[b, 1:w_buf, :]
        nk_ref[b, w_buf - 1:w_buf, :] = k_ref[pl.ds(b, 1), :]
        nv_ref[b, w_buf - 1:w_buf, :] = v_ref[pl.ds(b, 1), :]
        return carry

    lax.fori_loop(0, x_ref.shape[0], shift, 0)


def _kv_sample(x, g, w, gk, tables, ck, cv, chunk):
    m = x.shape[0]
    w_buf = ck.shape[1]
    row = pl.BlockSpec((chunk, D_MODEL), lambda i: (i, 0))
    tab = pl.BlockSpec((chunk, LANES), lambda i: (i, 0))
    new = pl.BlockSpec((chunk, KV_DIM), lambda i: (i, 0))
    cache = pl.BlockSpec((chunk, w_buf, KV_DIM), lambda i: (i, 0, 0))
    return pl.pallas_call(
        _kv_sample_body, grid=(m // chunk,),
        in_specs=[row, _const_spec((1, D_MODEL)), _const_spec((D_MODEL, 2 * KV_DIM)),
                  _const_spec((1, KV_DIM)), tab, tab, tab, cache, cache],
        out_specs=[new, new, cache, cache],
        out_shape=[jax.ShapeDtypeStruct((m, KV_DIM), F32)] * 2
        + [jax.ShapeDtypeStruct(ck.shape, F32)] * 2,
        compiler_params=_params(1), name="kv_sample")(x, g, w, gk, *tables, ck, cv)


def _queries(x, g, wq, gq, cos, sin_lo, sin_hi):
    q = _dot(_rms(x, g), wq)
    return _head_norm_rope(q, gq * (LOG2E / math.sqrt(HEAD_DIM)), cos, sin_lo, sin_hi)


def _attn_prompt_body(sinks_ref, x_ref, kc_ref, kp_ref, vtc_ref, vtp_ref, g_ref, wq_ref, gq_ref,
                      cos_ref, slo_ref, shi_ref, wo_ref, o_ref, q_scr, o_scr):
    tm = x_ref.shape[0]
    first_tile = pl.program_id(1) == 0
    x = x_ref[...]
    q_scr[...] = _queries(x, g_ref[...], wq_ref[...], gq_ref[...],
                          cos_ref[...], slo_ref[...], shi_ref[...]).astype(BF16)
    kcat = jnp.concatenate([kp_ref[...], kc_ref[...]], axis=0).astype(BF16)
    vtcat = jnp.concatenate([vtp_ref[...], vtc_ref[...]], axis=1).astype(BF16)

    seg = lax.broadcasted_iota(jnp.int32, (1, KV_DIM), 1) // HEAD_DIM
    kj = lax.broadcasted_iota(jnp.int32, (2 * WINDOW, WINDOW), 0)
    qi = lax.broadcasted_iota(jnp.int32, (2 * WINDOW, WINDOW), 1)
    rel = qi + WINDOW - kj
    band = (rel >= 0) & (rel < WINDOW)
    band0 = band & ((kj >= WINDOW) | jnp.logical_not(first_tile))
    bias = jnp.where(band, 0.0, -jnp.inf).astype(F32)
    bias0 = jnp.where(band0, 0.0, -jnp.inf).astype(F32)
    no_rows = jnp.zeros((HEAD_DIM, 2 * WINDOW), BF16)

    for i in range(tm // WINDOW):
        rows = slice(WINDOW * i, WINDOW * (i + 1))
        kw = kcat[WINDOW * i:WINDOW * (i + 2)]
        vtw = vtcat[:, WINDOW * i:WINDOW * (i + 2)]
        q4 = jnp.concatenate(
            [q_scr[rows, KV_DIM * g:KV_DIM * (g + 1)] for g in range(GROUP)], axis=0)
        bias4 = jnp.concatenate([bias0 if i == 0 else bias] * GROUP, axis=1)
        probs, vts, rdens = [], [], []
        for h in range(N_KV_HEADS):
            s = _dot_nt(jnp.where(seg == h, kw, jnp.zeros_like(kw)), q4) + bias4
            sink = jnp.concatenate(
                [jnp.full((1, WINDOW), sinks_ref[N_KV_HEADS * g + h] * LOG2E, F32)
                 for g in range(GROUP)], axis=1)
            m = jnp.maximum(jnp.max(s, axis=0, keepdims=True), sink)
            p = jnp.exp2(s - m)
            rdens.append(1.0 / (jnp.sum(p, axis=0, keepdims=True) + jnp.exp2(sink - m)))
            probs.append(p.astype(BF16))
            vts.append(jnp.concatenate(
                [vtw[HEAD_DIM * h:HEAD_DIM * (h + 1)] if hh == h else no_rows
                 for hh in range(N_KV_HEADS)], axis=0))
        o_t = _dot(jnp.concatenate(vts, axis=1), jnp.concatenate(probs, axis=0))
        o_t = jnp.concatenate(
            [o_t[HEAD_DIM * h:HEAD_DIM * (h + 1)] * rdens[h] for h in range(N_KV_HEADS)], axis=0)
        for g in range(GROUP):
            o_scr[rows, KV_DIM * g:KV_DIM * (g + 1)] = (
                o_t[:, WINDOW * g:WINDOW * (g + 1)].T)

    o_ref[...] = x + _dot(o_scr[...], wo_ref[...])


def _attn_prompt(x, k, vt, g, wq, gq, tables, sinks, wo, layer, tm):
    nb, s, _ = x.shape
    per_tile = tm // WINDOW
    row = pl.BlockSpec((None, tm, D_MODEL), lambda b, t: (b, t, 0))
    cur = pl.BlockSpec((None, tm, KV_DIM), lambda b, t: (b, t, 0))
    prev = pl.BlockSpec((None, WINDOW, KV_DIM),
                        lambda b, t: (b, jnp.maximum(t * per_tile - 1, 0), 0))
    cur_t = pl.BlockSpec((None, KV_DIM, tm), lambda b, t: (b, 0, t))
    prev_t = pl.BlockSpec((None, KV_DIM, WINDOW),
                          lambda b, t: (b, 0, jnp.maximum(t * per_tile - 1, 0)))
    tab = pl.BlockSpec((tm, LANES), lambda b, t: (t, 0))
    return pl.pallas_call(
        _attn_prompt_body, grid=(nb, s // tm),
        in_specs=[pl.BlockSpec(memory_space=pltpu.SMEM), row, cur, prev, cur_t, prev_t,
                  _const_spec((1, D_MODEL)), _layer_spec((D_MODEL, D_MODEL), layer),
                  _const_spec((1, D_MODEL)), tab, tab, tab, _layer_spec((D_MODEL, D_MODEL), layer)],
        out_specs=row,
        out_shape=jax.ShapeDtypeStruct(x.shape, F32),
        scratch_shapes=[pltpu.VMEM((tm, D_MODEL), BF16), pltpu.VMEM((tm, D_MODEL), F32)],
        compiler_params=_params(2), name="attn_prompt")(
            sinks, x, k, k, vt, vt, g, wq, gq, *tables, wo)


def _attn_sample_body(sinks_ref, x_ref, ck_ref, cv_ref, kn_ref, vn_ref, g_ref, wq_ref, gq_ref,
                      cos_ref, slo_ref, shi_ref, wo_ref, o_ref, q_scr, s_scr, p_scr, o_scr):
    chunk = x_ref.shape[0]
    x = x_ref[...]
    q_scr[...] = _queries(x, g_ref[...], wq_ref[...], gq_ref[...],
                          cos_ref[...], slo_ref[...], shi_ref[...])
    w_buf = ck_ref.shape[1]
    seg_rows = (lax.broadcasted_iota(jnp.int32, (N_KV_HEADS, KV_DIM), 1) // HEAD_DIM
                == lax.broadcasted_iota(jnp.int32, (N_KV_HEADS, KV_DIM), 0))
    expired = lax.broadcasted_iota(jnp.int32, (w_buf, 1), 0) == 0

    def scores(b, carry):
        kb = jnp.where(expired, kn_ref[pl.ds(b, 1), :], ck_ref[b]).astype(BF16)
        qrow = q_scr[pl.ds(b, 1), :]
        qrows = jnp.concatenate(
            [jnp.where(seg_rows, jnp.broadcast_to(qrow[:, KV_DIM * g:KV_DIM * (g + 1)],
                                                  (N_KV_HEADS, KV_DIM)), 0.0)
             for g in range(GROUP)], axis=0)
        s_scr[pl.ds(pl.multiple_of(b * N_HEADS, N_HEADS), N_HEADS), :] = _dot_nt(
            qrows.astype(BF16), kb)
        return carry

    lax.fori_loop(0, chunk, scores, 0, unroll=SAMPLE_UNROLL)

    s = s_scr[...]
    sink = jnp.concatenate(
        [jnp.full((1, 1), sinks_ref[i] * LOG2E, F32) for i in range(N_HEADS)] * chunk, axis=0)
    m = jnp.maximum(jnp.max(s, axis=-1, keepdims=True), sink)
    p = jnp.exp2(s - m)
    den = jnp.sum(p, axis=-1, keepdims=True) + jnp.exp2(sink - m)
    p_scr[...] = (p * (1.0 / den)).astype(BF16)

    def outputs(b, carry):
        vb = jnp.where(expired, vn_ref[pl.ds(b, 1), :], cv_ref[b]).astype(BF16)
        r = _dot(p_scr[pl.ds(pl.multiple_of(b * N_HEADS, N_HEADS), N_HEADS), :], vb)
        o_scr[pl.ds(b, 1), :] = jnp.concatenate(
            [jnp.sum(jnp.where(seg_rows, r[N_KV_HEADS * g:N_KV_HEADS * (g + 1)], 0.0),
                     axis=0, keepdims=True) for g in range(GROUP)], axis=1)
        return carry

    lax.fori_loop(0, chunk, outputs, 0, unroll=SAMPLE_UNROLL)
    o_ref[...] = x + _dot(o_scr[...], wo_ref[...])


def _attn_sample(x, ck, cv, kn, vn, g, wq, gq, tables, sinks, wo, layer, chunk):
    m = x.shape[0]
    w_buf = ck.shape[1]
    row = pl.BlockSpec((chunk, D_MODEL), lambda i: (i, 0))
    tab = pl.BlockSpec((chunk, LANES), lambda i: (i, 0))
    new = pl.BlockSpec((chunk, KV_DIM), lambda i: (i, 0))
    cache = pl.BlockSpec((chunk, w_buf, KV_DIM), lambda i: (i, 0, 0))
    return pl.pallas_call(
        _attn_sample_body, grid=(m // chunk,),
        in_specs=[pl.BlockSpec(memory_space=pltpu.SMEM), row, cache, cache, new, new,
                  _const_spec((1, D_MODEL)), _layer_spec((D_MODEL, D_MODEL), layer),
                  _const_spec((1, D_MODEL)), tab, tab, tab, _layer_spec((D_MODEL, D_MODEL), layer)],
        out_specs=row,
        out_shape=jax.ShapeDtypeStruct(x.shape, F32),
        scratch_shapes=[pltpu.VMEM((chunk, D_MODEL), F32),
                        pltpu.VMEM((chunk * N_HEADS, w_buf), F32),
                        pltpu.VMEM((chunk * N_HEADS, w_buf), BF16),
                        pltpu.VMEM((chunk, D_MODEL), F32)],
        compiler_params=_params(1), name="attn_sample")(
            sinks, x, ck, cv, kn, vn, g, wq, gq, *tables, wo)


def _rope_tables(pos):
    half = ROT_DIM // 2
    inv_freq = ROPE_THETA ** (-jnp.arange(0, ROT_DIM, 2, dtype=F32) / ROT_DIM)
    ang = pos.astype(F32)[:, None] * inv_freq[None, :]
    cos, sin = jnp.cos(ang), jnp.sin(ang)
    n = pos.shape[0]
    rest = HEAD_DIM - ROT_DIM
    cos_h = jnp.concatenate([cos, cos, jnp.ones((n, rest), F32)], axis=1)
    sin_lo = jnp.concatenate([-sin, jnp.zeros((n, half + rest), F32)], axis=1)
    sin_hi = jnp.concatenate([jnp.zeros((n, half), F32), sin, jnp.zeros((n, rest), F32)], axis=1)
    reps = LANES // HEAD_DIM
    return tuple(jnp.tile(t, (1, reps)) for t in (cos_h, sin_lo, sin_hi))


def kernel(x_prompt, x_sample, state_conv, cache_k, cache_v, g_ffn1, w_ffn1_gate, w_ffn1_up,
           w_ffn1_down, g_mix, g_ffn2, w_ffn2_gate, w_ffn2_up, w_ffn2_down, w_in_a, conv_w,
           w_out_a, g_kv, w_kv, g_knorm, w_q, g_qnorm, sinks, w_o):
    nb, seq, _ = x_prompt.shape
    nd, dec_seq, _ = x_sample.shape
    assert dec_seq == 1 and cache_k.shape[1] == WINDOW
    depth = g_ffn1.shape[0]
    n_a = w_in_a.shape[0]
    past_len = seq
    w_buf = cache_k.shape[1]

    n_b = w_q.shape[0]
    ffn1 = (w_ffn1_gate, w_ffn1_up, w_ffn1_down)
    ffn2 = (w_ffn2_gate, w_ffn2_up, w_ffn2_down)
    w_in, w_out, wkv = w_in_a, w_out_a, w_kv
    wq = (w_q.reshape(n_b, D_MODEL, N_KV_HEADS, GROUP, HEAD_DIM).transpose(0, 1, 3, 2, 4)
          .reshape(n_b, D_MODEL, D_MODEL))
    wo = (w_o.reshape(n_b, N_KV_HEADS, GROUP, HEAD_DIM, D_MODEL).transpose(0, 2, 1, 3, 4)
          .reshape(n_b, D_MODEL, D_MODEL))
    sinks_gm = sinks.reshape(n_b, N_KV_HEADS, GROUP).transpose(0, 2, 1).reshape(n_b, N_HEADS)
    tab_p = _rope_tables(jnp.arange(seq, dtype=jnp.int32))
    tab_s = _rope_tables(jnp.full((nd,), past_len, jnp.int32))
    gk = jnp.tile(g_knorm, N_KV_HEADS)[None]

    xp = x_prompt
    xs = x_sample.reshape(nd, D_MODEL)
    ck = cache_k.reshape(nd, w_buf, KV_DIM)
    cv = cache_v.reshape(nd, w_buf, KV_DIM)
    conv_p, conv_s = [], []
    kp = vp = vtp = kn = vn = new_ck = new_cv = None

    def ffn_both(xp, xs, g, weights, layer):
        xp, xs = _ffn(xp.reshape(nb * seq, D_MODEL), xs, g[None], *weights, layer, TM_FFN)
        return xp.reshape(nb, seq, D_MODEL), xs

    for i in range(depth):
        if i == n_a:
            kp, vp, vtp = _kv_prompt(xp, g_kv[None], wkv, gk, tab_p, TM_KV)
            kn, vn, new_ck, new_cv = _kv_sample(xs, g_kv[None], wkv, gk, tab_s, ck, cv,
                                                SAMPLE_CHUNK)
        xp, xs = ffn_both(xp, xs, g_ffn1[i], ffn1, i)
        gm = g_mix[i][None]
        if i < n_a:
            xp, st_p, xs, st_s = _conv(
                xp, jnp.zeros((nb, CONV_W - 1, D_MODEL), F32), xs,
                state_conv[i].reshape(nd, (CONV_W - 1) * D_MODEL), gm, w_in, conv_w[i], w_out,
                i, TM_CONV)
            conv_p.append(st_p)
            conv_s.append(st_s.reshape(nd, CONV_W - 1, D_MODEL))
        else:
            j = i - n_a
            gq = jnp.tile(g_qnorm[j], N_HEADS)[None]
            xp = _attn_prompt(xp, kp, vtp, gm, wq, gq, tab_p, sinks_gm[j], wo, j, TM_ATTN)
            xs = _attn_sample(xs, ck, cv, kn, vn, gm, wq, gq, tab_s, sinks_gm[j], wo, j,
                              SAMPLE_CHUNK)
        xp, xs = ffn_both(xp, xs, g_ffn2[i], ffn2, i)

    kv_shape = (w_buf, N_KV_HEADS, HEAD_DIM)
    return (xp, xs.reshape(nd, 1, D_MODEL), jnp.stack(conv_p),
            kp[:, seq - w_buf:].reshape(nb, *kv_shape), vp[:, seq - w_buf:].reshape(nb, *kv_shape),
            jnp.stack(conv_s), new_ck.reshape(nd, *kv_shape), new_cv.reshape(nd, *kv_shape))
```

```python
import functools
import math

import jax
import jax.numpy as jnp
from jax import lax
from jax.experimental import pallas as pl
from jax.experimental.pallas import tpu as pltpu

D_MODEL = 1024
D_FF = 2816
N_HEADS = 16
N_KV_HEADS = 4
GROUP = N_HEADS // N_KV_HEADS
HEAD_DIM = 64
KV_DIM = N_KV_HEADS * HEAD_DIM
ROT_DIM = HEAD_DIM // 4
ROPE_THETA = 500000.0
WINDOW = 128
EPS = 1e-6
LOG2E = math.log2(math.e)
CONV_W = 3

LANES = 128
SUBLANES = 8
VMEM_LIMIT = 56 * 1024 * 1024

FFN_CHUNK = 256
TM_FFN = 512
TM_CONV = 1024
TM_KV = 1024
TM_ATTN = 1024
SAMPLE_CHUNK = 32
SAMPLE_UNROLL = 4

F32 = jnp.float32
BF16 = jnp.bfloat16


def _const_spec(shape):
    return pl.BlockSpec(shape, lambda *_: (0,) * len(shape), pipeline_mode=pl.Buffered(1))


def _layer_spec(shape, layer):
    return pl.BlockSpec((None,) + tuple(shape), lambda *_: (layer,) + (0,) * len(shape),
                        pipeline_mode=pl.Buffered(1))


def _params(n_axes):
    return pltpu.CompilerParams(dimension_semantics=("arbitrary",) * n_axes,
                                vmem_limit_bytes=VMEM_LIMIT)


def _dot(a, b):
    return jnp.dot(a, b, preferred_element_type=F32)


def _dot_nt(a, b):
    return lax.dot_general(a, b, (((1,), (1,)), ((), ())), preferred_element_type=F32)


def _rms(x, g):
    ms = jnp.mean(x * x, axis=-1, keepdims=True)
    return x * lax.rsqrt(ms + EPS) * g


def _head_mean_matrix():
    r = lax.broadcasted_iota(jnp.int32, (LANES, LANES), 0) // HEAD_DIM
    c = lax.broadcasted_iota(jnp.int32, (LANES, LANES), 1) // HEAD_DIM
    return jnp.where(r == c, 1.0 / HEAD_DIM, 0.0).astype(F32)


def _head_norm_rope(x, gain, cos, sin_lo, sin_hi):
    bd = _head_mean_matrix()
    outs = []
    for j in range(x.shape[1] // LANES):
        xs = x[:, LANES * j:LANES * (j + 1)]
        ms = _dot(xs * xs, bd)
        xn = xs * lax.rsqrt(ms + EPS) * gain[:, LANES * j:LANES * (j + 1)]
        outs.append(xn * cos + pltpu.roll(xn, ROT_DIM // 2, 1) * sin_hi
                    + pltpu.roll(xn, LANES - ROT_DIM // 2, 1) * sin_lo)
    return jnp.concatenate(outs, axis=1)


def _ffn_rows(x, g, wg_ref, wu_ref, wd_ref):
    h = _rms(x, g)
    acc = jnp.zeros_like(x)
    for c in range(D_FF // FFN_CHUNK):
        sl = slice(c * FFN_CHUNK, (c + 1) * FFN_CHUNK)
        gate = _dot(h, wg_ref[:, sl])
        up = _dot(h, wu_ref[:, sl])
        a = gate * jax.nn.sigmoid(gate) * up
        acc = acc + _dot(a, wd_ref[sl, :])
    return x + 0.5 * acc


def _ffn_body(xp_ref, xs_ref, g_ref, wg_ref, wu_ref, wd_ref, op_ref, os_ref):
    step = pl.program_id(0)
    last = pl.num_programs(0) - 1

    @pl.when(step < last)
    def _():
        op_ref[...] = _ffn_rows(xp_ref[...], g_ref[...], wg_ref, wu_ref, wd_ref)

    @pl.when(step == last)
    def _():
        os_ref[...] = _ffn_rows(xs_ref[...], g_ref[...], wg_ref, wu_ref, wd_ref)


def _ffn(xp, xs, g, wg, wu, wd, layer, tm):
    m, ms = xp.shape[0], xs.shape[0]
    n_p = m // tm
    row_p = pl.BlockSpec((tm, D_MODEL), lambda i: (jnp.minimum(i, n_p - 1), 0))
    return pl.pallas_call(
        _ffn_body, grid=(n_p + 1,),
        in_specs=[row_p, _const_spec((ms, D_MODEL)), _const_spec((1, D_MODEL)),
                  _layer_spec((D_MODEL, D_FF), layer), _layer_spec((D_MODEL, D_FF), layer),
                  _layer_spec((D_FF, D_MODEL), layer)],
        out_specs=[row_p, pl.BlockSpec((ms, D_MODEL), lambda i: (0, 0))],
        out_shape=[jax.ShapeDtypeStruct((m, D_MODEL), F32),
                   jax.ShapeDtypeStruct((ms, D_MODEL), F32)],
        compiler_params=_params(1), name="ffn")(xp, xs, g, wg, wu, wd)


def _conv_project(x, g, win_ref):
    bcu = _dot(_rms(x, g), win_ref[...])
    return bcu[:, :D_MODEL], bcu[:, D_MODEL:2 * D_MODEL] * bcu[:, 2 * D_MODEL:]


def _conv_body(xp_ref, stp_ref, xs_ref, sts_ref, g_ref, win_ref, cw_ref, wout_ref,
               op_ref, nstp_ref, os_ref, nsts_ref, ext_ref, *, tiles_per_seq):
    step = pl.program_id(0)
    last = pl.num_programs(0) - 1
    cw = cw_ref[...]

    @pl.when(step < last)
    def _():
        tm = xp_ref.shape[0]
        lead = SUBLANES - (CONV_W - 1)

        @pl.when(lax.rem(step, tiles_per_seq) == 0)
        def _():
            ext_ref[lead:SUBLANES, :] = stp_ref[...]

        x = xp_ref[...]
        b, cu = _conv_project(x, g_ref[...], win_ref)
        ext_ref[SUBLANES:SUBLANES + tm, :] = cu
        conv = (cw[0:1] * ext_ref[lead:lead + tm, :]
                + cw[1:2] * ext_ref[lead + 1:lead + 1 + tm, :] + cw[2:3] * cu)
        op_ref[...] = x + _dot(b * conv, wout_ref[...])
        tail = ext_ref[lead + tm:SUBLANES + tm, :]
        ext_ref[lead:SUBLANES, :] = tail
        nstp_ref[...] = tail

    @pl.when(step == last)
    def _():
        x = xs_ref[...]
        b, cu = _conv_project(x, g_ref[...], win_ref)
        s1 = sts_ref[:, D_MODEL:]
        conv = cw[0:1] * sts_ref[:, :D_MODEL] + cw[1:2] * s1 + cw[2:3] * cu
        os_ref[...] = x + _dot(b * conv, wout_ref[...])
        nsts_ref[:, :D_MODEL] = s1
        nsts_ref[:, D_MODEL:] = cu


def _conv(xp, state_p, xs, state_s, g, w_in, cw, w_out, layer, tm):
    nb, s, _ = xp.shape
    ms = xs.shape[0]
    nt = s // tm
    n_p = nb * nt

    def tile(i):
        return jnp.minimum(i, n_p - 1)

    row = pl.BlockSpec((None, tm, D_MODEL), lambda i: (tile(i) // nt, tile(i) % nt, 0))
    st = pl.BlockSpec((None, CONV_W - 1, D_MODEL), lambda i: (tile(i) // nt, 0, 0))
    return pl.pallas_call(
        functools.partial(_conv_body, tiles_per_seq=nt), grid=(n_p + 1,),
        in_specs=[row, st, _const_spec((ms, D_MODEL)), _const_spec((ms, 2 * D_MODEL)),
                  _const_spec((1, D_MODEL)), _layer_spec((D_MODEL, 3 * D_MODEL), layer),
                  _const_spec((CONV_W, D_MODEL)), _layer_spec((D_MODEL, D_MODEL), layer)],
        out_specs=[row, st, pl.BlockSpec((ms, D_MODEL), lambda i: (0, 0)),
                   pl.BlockSpec((ms, 2 * D_MODEL), lambda i: (0, 0))],
        out_shape=[jax.ShapeDtypeStruct(xp.shape, F32),
                   jax.ShapeDtypeStruct((nb, CONV_W - 1, D_MODEL), F32),
                   jax.ShapeDtypeStruct((ms, D_MODEL), F32),
                   jax.ShapeDtypeStruct((ms, 2 * D_MODEL), F32)],
        scratch_shapes=[pltpu.VMEM((tm + SUBLANES, D_MODEL), F32)],
        compiler_params=_params(1), name="conv")(xp, state_p, xs, state_s, g, w_in, cw, w_out)


def _kv_rows(x, g, w, gk, cos, sin_lo, sin_hi):
    kv = _dot(_rms(x, g), w)
    k = _head_norm_rope(kv[:, :KV_DIM], gk, cos, sin_lo, sin_hi)
    return k, kv[:, KV_DIM:]


def _kv_prompt_body(x_ref, g_ref, w_ref, gk_ref, cos_ref, slo_ref, shi_ref,
                    k_ref, vt_ref, klast_ref, vlast_ref):
    k, v = _kv_rows(x_ref[...], g_ref[...], w_ref[...], gk_ref[...],
                    cos_ref[...], slo_ref[...], shi_ref[...])
    k_ref[...] = k.astype(BF16)
    vt_ref[...] = v.T.astype(BF16)
    tm = k.shape[0]
    klast_ref[...] = k[tm - WINDOW:]
    vlast_ref[...] = v[tm - WINDOW:]


def _kv_prompt(x, g, w, gk, tables, tm):
    nb, s, _ = x.shape
    row = pl.BlockSpec((None, tm, D_MODEL), lambda b, t: (b, t, 0))
    tab = pl.BlockSpec((tm, LANES), lambda b, t: (t, 0))
    out = pl.BlockSpec((None, tm, KV_DIM), lambda b, t: (b, t, 0))
    out_t = pl.BlockSpec((None, KV_DIM, tm), lambda b, t: (b, 0, t))
    last = pl.BlockSpec((None, WINDOW, KV_DIM), lambda b, t: (b, 0, 0))
    return pl.pallas_call(
        _kv_prompt_body, grid=(nb, s // tm),
        in_specs=[row, _const_spec((1, D_MODEL)), _const_spec((D_MODEL, 2 * KV_DIM)),
                  _const_spec((1, KV_DIM)), tab, tab, tab],
        out_specs=[out, out_t, last, last],
        out_shape=[jax.ShapeDtypeStruct((nb, s, KV_DIM), BF16),
                   jax.ShapeDtypeStruct((nb, KV_DIM, s), BF16)]
        + [jax.ShapeDtypeStruct((nb, WINDOW, KV_DIM), F32)] * 2,
        compiler_params=_params(2), name="kv_prompt")(x, g, w, gk, *tables)


def _kv_sample_body(x_ref, g_ref, w_ref, gk_ref, cos_ref, slo_ref, shi_ref, ck_ref, cv_ref,
                    k_ref, v_ref, nk_ref, nv_ref):
    k, v = _kv_rows(x_ref[...], g_ref[...], w_ref[...], gk_ref[...],
                    cos_ref[...], slo_ref[...], shi_ref[...])
    k_ref[...] = k
    v_ref[...] = v
    w_buf = ck_ref.shape[1]

    def shift(b, carry):
        nk_ref[b, 0:w_buf - 1, :] = ck_ref[b, 1:w_buf, :]
        nv_ref[b, 0:w_buf - 1, :] = cv_ref[b, 1:w_buf, :]
        nk_ref[b, w_buf - 1:w_buf, :] = k_ref[pl.ds(b, 1), :]
        nv_ref[b, w_buf - 1:w_buf, :] = v_ref[pl.ds(b, 1), :]
        return carry

    lax.fori_loop(0, x_ref.shape[0], shift, 0)


def _kv_sample(x, g, w, gk, tables, ck, cv, chunk):
    m = x.shape[0]
    w_buf = ck.shape[1]
    row = pl.BlockSpec((chunk, D_MODEL), lambda i: (i, 0))
    tab = pl.BlockSpec((chunk, LANES), lambda i: (i, 0))
    new = pl.BlockSpec((chunk, KV_DIM), lambda i: (i, 0))
    cache = pl.BlockSpec((chunk, w_buf, KV_DIM), lambda i: (i, 0, 0))
    return pl.pallas_call(
        _kv_sample_body, grid=(m // chunk,),
        in_specs=[row, _const_spec((1, D_MODEL)), _const_spec((D_MODEL, 2 * KV_DIM)),
                  _const_spec((1, KV_DIM)), tab, tab, tab, cache, cache],
        out_specs=[new, new, cache, cache],
        out_shape=[jax.ShapeDtypeStruct((m, KV_DIM), F32)] * 2
        + [jax.ShapeDtypeStruct(ck.shape, F32)] * 2,
        compiler_params=_params(1), name="kv_sample")(x, g, w, gk, *tables, ck, cv)


def _queries(x, g, wq, gq, cos, sin_lo, sin_hi):
    q = _dot(_rms(x, g), wq)
    return _head_norm_rope(q, gq * (LOG2E / math.sqrt(HEAD_DIM)), cos, sin_lo, sin_hi)


def _attn_prompt_body(sinks_ref, x_ref, kc_ref, kp_ref, vtc_ref, vtp_ref, g_ref, wq_ref, gq_ref,
                      cos_ref, slo_ref, shi_ref, wo_ref, o_ref, q_scr, o_scr):
    tm = x_ref.shape[0]
    first_tile = pl.program_id(1) == 0
    x = x_ref[...]
    q_scr[...] = _queries(x, g_ref[...], wq_ref[...], gq_ref[...],
                          cos_ref[...], slo_ref[...], shi_ref[...]).astype(BF16)
    kcat = jnp.concatenate([kp_ref[...], kc_ref[...]], axis=0)
    vtcat = jnp.concatenate([vtp_ref[...], vtc_ref[...]], axis=1)

    seg = lax.broadcasted_iota(jnp.int32, (1, KV_DIM), 1) // HEAD_DIM
    kj = lax.broadcasted_iota(jnp.int32, (2 * WINDOW, WINDOW), 0)
    qi = lax.broadcasted_iota(jnp.int32, (2 * WINDOW, WINDOW), 1)
    rel = qi + WINDOW - kj
    band = (rel >= 0) & (rel < WINDOW)
    band0 = band & ((kj >= WINDOW) | jnp.logical_not(first_tile))
    bias = jnp.where(band, 0.0, -jnp.inf).astype(F32)
    bias0 = jnp.where(band0, 0.0, -jnp.inf).astype(F32)
    no_rows = jnp.zeros((HEAD_DIM, 2 * WINDOW), BF16)

    for i in range(tm // WINDOW):
        rows = slice(WINDOW * i, WINDOW * (i + 1))
        kw = kcat[WINDOW * i:WINDOW * (i + 2)]
        vtw = vtcat[:, WINDOW * i:WINDOW * (i + 2)]
        q4 = jnp.concatenate(
            [q_scr[rows, KV_DIM * g:KV_DIM * (g + 1)] for g in range(GROUP)], axis=0)
        bias4 = jnp.concatenate([bias0 if i == 0 else bias] * GROUP, axis=1)
        probs, vts, rdens = [], [], []
        for h in range(N_KV_HEADS):
            s = _dot_nt(jnp.where(seg == h, kw, jnp.zeros_like(kw)), q4) + bias4
            sink = jnp.concatenate(
                [jnp.full((1, WINDOW), sinks_ref[N_KV_HEADS * g + h] * LOG2E, F32)
                 for g in range(GROUP)], axis=1)
            m = jnp.maximum(jnp.max(s, axis=0, keepdims=True), sink)
            p = jnp.exp2(s - m)
            rdens.append(1.0 / (jnp.sum(p, axis=0, keepdims=True) + jnp.exp2(sink - m)))
            probs.append(p.astype(BF16))
            vts.append(jnp.concatenate(
                [vtw[HEAD_DIM * h:HEAD_DIM * (h + 1)] if hh == h else no_rows
                 for hh in range(N_KV_HEADS)], axis=0))
        o_t = _dot(jnp.concatenate(vts, axis=1), jnp.concatenate(probs, axis=0))
        o_t = jnp.concatenate(
            [o_t[HEAD_DIM * h:HEAD_DIM * (h + 1)] * rdens[h] for h in range(N_KV_HEADS)], axis=0)
        for g in range(GROUP):
            o_scr[rows, KV_DIM * g:KV_DIM * (g + 1)] = (
                o_t[:, WINDOW * g:WINDOW * (g + 1)].T)

    o_ref[...] = x + _dot(o_scr[...], wo_ref[...])


def _attn_prompt(x, k, vt, g, wq, gq, tables, sinks, wo, layer, tm):
    nb, s, _ = x.shape
    per_tile = tm // WINDOW
    row = pl.BlockSpec((None, tm, D_MODEL), lambda b, t: (b, t, 0))
    cur = pl.BlockSpec((None, tm, KV_DIM), lambda b, t: (b, t, 0))
    prev = pl.BlockSpec((None, WINDOW, KV_DIM),
                        lambda b, t: (b, jnp.maximum(t * per_tile - 1, 0), 0))
    cur_t = pl.BlockSpec((None, KV_DIM, tm), lambda b, t: (b, 0, t))
    prev_t = pl.BlockSpec((None, KV_DIM, WINDOW),
                          lambda b, t: (b, 0, jnp.maximum(t * per_tile - 1, 0)))
    tab = pl.BlockSpec((tm, LANES), lambda b, t: (t, 0))
    return pl.pallas_call(
        _attn_prompt_body, grid=(nb, s // tm),
        in_specs=[pl.BlockSpec(memory_space=pltpu.SMEM), row, cur, prev, cur_t, prev_t,
                  _const_spec((1, D_MODEL)), _layer_spec((D_MODEL, D_MODEL), layer),
                  _const_spec((1, D_MODEL)), tab, tab, tab, _layer_spec((D_MODEL, D_MODEL), layer)],
        out_specs=row,
        out_shape=jax.ShapeDtypeStruct(x.shape, F32),
        scratch_shapes=[pltpu.VMEM((tm, D_MODEL), BF16), pltpu.VMEM((tm, D_MODEL), F32)],
        compiler_params=_params(2), name="attn_prompt")(
            sinks, x, k, k, vt, vt, g, wq, gq, *tables, wo)


def _attn_sample_body(sinks_ref, x_ref, ck_ref, cv_ref, kn_ref, vn_ref, g_ref, wq_ref, gq_ref,
                      cos_ref, slo_ref, shi_ref, wo_ref, o_ref, q_scr, s_scr, p_scr, o_scr):
    chunk = x_ref.shape[0]
    x = x_ref[...]
    q_scr[...] = _queries(x, g_ref[...], wq_ref[...], gq_ref[...],
                          cos_ref[...], slo_ref[...], shi_ref[...])
    w_buf = ck_ref.shape[1]
    seg_rows = (lax.broadcasted_iota(jnp.int32, (N_KV_HEADS, KV_DIM), 1) // HEAD_DIM
                == lax.broadcasted_iota(jnp.int32, (N_KV_HEADS, KV_DIM), 0))
    expired = lax.broadcasted_iota(jnp.int32, (w_buf, 1), 0) == 0

    def scores(b, carry):
        kb = jnp.where(expired, kn_ref[pl.ds(b, 1), :], ck_ref[b]).astype(BF16)
        qrow = q_scr[pl.ds(b, 1), :]
        qrows = jnp.concatenate(
            [jnp.where(seg_rows, jnp.broadcast_to(qrow[:, KV_DIM * g:KV_DIM * (g + 1)],
                                                  (N_KV_HEADS, KV_DIM)), 0.0)
             for g in range(GROUP)], axis=0)
        s_scr[pl.ds(pl.multiple_of(b * N_HEADS, N_HEADS), N_HEADS), :] = _dot_nt(
            qrows.astype(BF16), kb)
        return carry

    lax.fori_loop(0, chunk, scores, 0, unroll=SAMPLE_UNROLL)

    s = s_scr[...]
    sink = jnp.concatenate(
        [jnp.full((1, 1), sinks_ref[i] * LOG2E, F32) for i in range(N_HEADS)] * chunk, axis=0)
    m = jnp.maximum(jnp.max(s, axis=-1, keepdims=True), sink)
    p = jnp.exp2(s - m)
    den = jnp.sum(p, axis=-1, keepdims=True) + jnp.exp2(sink - m)
    p_scr[...] = (p * (1.0 / den)).astype(BF16)

    def outputs(b, carry):
        vb = jnp.where(expired, vn_ref[pl.ds(b, 1), :], cv_ref[b]).astype(BF16)
        r = _dot(p_scr[pl.ds(pl.multiple_of(b * N_HEADS, N_HEADS), N_HEADS), :], vb)
        o_scr[pl.ds(b, 1), :] = jnp.concatenate(
            [jnp.sum(jnp.where(seg_rows, r[N_KV_HEADS * g:N_KV_HEADS * (g + 1)], 0.0),
                     axis=0, keepdims=True) for g in range(GROUP)], axis=1)
        return carry

    lax.fori_loop(0, chunk, outputs, 0, unroll=SAMPLE_UNROLL)
    o_ref[...] = x + _dot(o_scr[...], wo_ref[...])


def _attn_sample(x, ck, cv, kn, vn, g, wq, gq, tables, sinks, wo, layer, chunk):
    m = x.shape[0]
    w_buf = ck.shape[1]
    row = pl.BlockSpec((chunk, D_MODEL), lambda i: (i, 0))
    tab = pl.BlockSpec((chunk, LANES), lambda i: (i, 0))
    new = pl.BlockSpec((chunk, KV_DIM), lambda i: (i, 0))
    cache = pl.BlockSpec((chunk, w_buf, KV_DIM), lambda i: (i, 0, 0))
    return pl.pallas_call(
        _attn_sample_body, grid=(m // chunk,),
        in_specs=[pl.BlockSpec(memory_space=pltpu.SMEM), row, cache, cache, new, new,
                  _const_spec((1, D_MODEL)), _layer_spec((D_MODEL, D_MODEL), layer),
                  _const_spec((1, D_MODEL)), tab, tab, tab, _layer_spec((D_MODEL, D_MODEL), layer)],
        out_specs=row,
        out_shape=jax.ShapeDtypeStruct(x.shape, F32),
        scratch_shapes=[pltpu.VMEM((chunk, D_MODEL), F32),
                        pltpu.VMEM((chunk * N_HEADS, w_buf), F32),
                        pltpu.VMEM((chunk * N_HEADS, w_buf), BF16),
                        pltpu.VMEM((chunk, D_MODEL), F32)],
        compiler_params=_params(1), name="attn_sample")(
            sinks, x, ck, cv, kn, vn, g, wq, gq, *tables, wo)


def _rope_tables(pos):
    half = ROT_DIM // 2
    inv_freq = ROPE_THETA ** (-jnp.arange(0, ROT_DIM, 2, dtype=F32) / ROT_DIM)
    ang = pos.astype(F32)[:, None] * inv_freq[None, :]
    cos, sin = jnp.cos(ang), jnp.sin(ang)
    n = pos.shape[0]
    rest = HEAD_DIM - ROT_DIM
    cos_h = jnp.concatenate([cos, cos, jnp.ones((n, rest), F32)], axis=1)
    sin_lo = jnp.concatenate([-sin, jnp.zeros((n, half + rest), F32)], axis=1)
    sin_hi = jnp.concatenate([jnp.zeros((n, half), F32), sin, jnp.zeros((n, rest), F32)], axis=1)
    reps = LANES // HEAD_DIM
    return tuple(jnp.tile(t, (1, reps)) for t in (cos_h, sin_lo, sin_hi))


def kernel(x_prompt, x_sample, state_conv, cache_k, cache_v, g_ffn1, w_ffn1_gate, w_ffn1_up,
           w_ffn1_down, g_mix, g_ffn2, w_ffn2_gate, w_ffn2_up, w_ffn2_down, w_in_a, conv_w,
           w_out_a, g_kv, w_kv, g_knorm, w_q, g_qnorm, sinks, w_o):
    nb, seq, _ = x_prompt.shape
    nd, dec_seq, _ = x_sample.shape
    assert dec_seq == 1 and cache_k.shape[1] == WINDOW
    depth = g_ffn1.shape[0]
    n_a = w_in_a.shape[0]
    past_len = seq
    w_buf = cache_k.shape[1]

    n_b = w_q.shape[0]
    ffn1 = (w_ffn1_gate, w_ffn1_up, w_ffn1_down)
    ffn2 = (w_ffn2_gate, w_ffn2_up, w_ffn2_down)
    w_in, w_out, wkv = w_in_a, w_out_a, w_kv
    wq = (w_q.reshape(n_b, D_MODEL, N_KV_HEADS, GROUP, HEAD_DIM).transpose(0, 1, 3, 2, 4)
          .reshape(n_b, D_MODEL, D_MODEL))
    wo = (w_o.reshape(n_b, N_KV_HEADS, GROUP, HEAD_DIM, D_MODEL).transpose(0, 2, 1, 3, 4)
          .reshape(n_b, D_MODEL, D_MODEL))
    sinks_gm = sinks.reshape(n_b, N_KV_HEADS, GROUP).transpose(0, 2, 1).reshape(n_b, N_HEADS)
    tab_p = _rope_tables(jnp.arange(seq, dtype=jnp.int32))
    tab_s = _rope_tables(jnp.full((nd,), past_len, jnp.int32))
    gk = jnp.tile(g_knorm, N_KV_HEADS)[None]

    xp = x_prompt
    xs = x_sample.reshape(nd, D_MODEL)
    ck = cache_k.reshape(nd, w_buf, KV_DIM)
    cv = cache_v.reshape(nd, w_buf, KV_DIM)
    conv_p, conv_s = [], []
    kp = vtp = k_last = v_last = kn = vn = new_ck = new_cv = None

    def ffn_both(xp, xs, g, weights, layer):
        xp, xs = _ffn(xp.reshape(nb * seq, D_MODEL), xs, g[None], *weights, layer, TM_FFN)
        return xp.reshape(nb, seq, D_MODEL), xs

    for i in range(depth):
        if i == n_a:
            kp, vtp, k_last, v_last = _kv_prompt(xp, g_kv[None], wkv, gk, tab_p, TM_KV)
            kn, vn, new_ck, new_cv = _kv_sample(xs, g_kv[None], wkv, gk, tab_s, ck, cv,
                                                SAMPLE_CHUNK)
        xp, xs = ffn_both(xp, xs, g_ffn1[i], ffn1, i)
        gm = g_mix[i][None]
        if i < n_a:
            xp, st_p, xs, st_s = _conv(
                xp, jnp.zeros((nb, CONV_W - 1, D_MODEL), F32), xs,
                state_conv[i].reshape(nd, (CONV_W - 1) * D_MODEL), gm, w_in, conv_w[i], w_out,
                i, TM_CONV)
            conv_p.append(st_p)
            conv_s.append(st_s.reshape(nd, CONV_W - 1, D_MODEL))
        else:
            j = i - n_a
            gq = jnp.tile(g_qnorm[j], N_HEADS)[None]
            xp = _attn_prompt(xp, kp, vtp, gm, wq, gq, tab_p, sinks_gm[j], wo, j, TM_ATTN)
            xs = _attn_sample(xs, ck, cv, kn, vn, gm, wq, gq, tab_s, sinks_gm[j], wo, j,
                              SAMPLE_CHUNK)
        xp, xs = ffn_both(xp, xs, g_ffn2[i], ffn2, i)

    kv_shape = (w_buf, N_KV_HEADS, HEAD_DIM)
    return (xp, xs.reshape(nd, 1, D_MODEL), jnp.stack(conv_p),
            k_last.reshape(nb, *kv_shape), v_last.reshape(nb, *kv_shape),
            jnp.stack(conv_s), new_ck.reshape(nd, *kv_shape), new_cv.reshape(nd, *kv_shape))
```

```python
import functools
import math

import jax
import jax.numpy as jnp
from jax import lax
from jax.experimental import pallas as pl
from jax.experimental.pallas import tpu as pltpu

D_MODEL = 1024
D_FF = 2816
N_HEADS = 16
N_KV_HEADS = 4
GROUP = N_HEADS // N_KV_HEADS
HEAD_DIM = 64
KV_DIM = N_KV_HEADS * HEAD_DIM
ROT_DIM = HEAD_DIM // 4
ROPE_THETA = 500000.0
WINDOW = 128
EPS = 1e-6
LOG2E = math.log2(math.e)
CONV_W = 3

LANES = 128
SUBLANES = 8
VMEM_LIMIT = 56 * 1024 * 1024

FFN_CHUNK = 256
TM_FFN = 512
TM_CONV = 1024
TM_KV = 1024
TM_ATTN = 1024
SAMPLE_CHUNK = 32
SAMPLE_UNROLL = 4

F32 = jnp.float32
BF16 = jnp.bfloat16


def _const_spec(shape):
    return pl.BlockSpec(shape, lambda *_: (0,) * len(shape), pipeline_mode=pl.Buffered(1))


def _layer_spec(shape, layer):
    return pl.BlockSpec((None,) + tuple(shape), lambda *_: (layer,) + (0,) * len(shape),
                        pipeline_mode=pl.Buffered(1))


def _params(n_axes):
    return pltpu.CompilerParams(dimension_semantics=("arbitrary",) * n_axes,
                                vmem_limit_bytes=VMEM_LIMIT)


def _dot(a, b):
    return jnp.dot(a, b, preferred_element_type=F32)


def _dot_nt(a, b):
    return lax.dot_general(a, b, (((1,), (1,)), ((), ())), preferred_element_type=F32)


def _rms(x, g):
    ms = jnp.mean(x * x, axis=-1, keepdims=True)
    return x * lax.rsqrt(ms + EPS) * g


def _head_mean_matrix():
    r = lax.broadcasted_iota(jnp.int32, (LANES, LANES), 0) // HEAD_DIM
    c = lax.broadcasted_iota(jnp.int32, (LANES, LANES), 1) // HEAD_DIM
    return jnp.where(r == c, 1.0 / HEAD_DIM, 0.0).astype(F32)


def _head_norm_rope(x, gain, cos, sin_lo, sin_hi):
    bd = _head_mean_matrix()
    outs = []
    for j in range(x.shape[1] // LANES):
        xs = x[:, LANES * j:LANES * (j + 1)]
        ms = _dot(xs * xs, bd)
        xn = xs * lax.rsqrt(ms + EPS) * gain[:, LANES * j:LANES * (j + 1)]
        outs.append(xn * cos + pltpu.roll(xn, ROT_DIM // 2, 1) * sin_hi
                    + pltpu.roll(xn, LANES - ROT_DIM // 2, 1) * sin_lo)
    return jnp.concatenate(outs, axis=1)


def _ffn_rows(x, g, wg_ref, wu_ref, wd_ref):
    h = _rms(x, g)
    acc = jnp.zeros_like(x)
    for c in range(D_FF // FFN_CHUNK):
        sl = slice(c * FFN_CHUNK, (c + 1) * FFN_CHUNK)
        gate = _dot(h, wg_ref[:, sl])
        up = _dot(h, wu_ref[:, sl])
        a = gate * jax.nn.sigmoid(gate) * up
        acc = acc + _dot(a, wd_ref[sl, :])
    return x + 0.5 * acc


def _ffn_body(xp_ref, xs_ref, g_ref, wg_ref, wu_ref, wd_ref, op_ref, os_ref):
    step = pl.program_id(0)
    last = pl.num_programs(0) - 1

    @pl.when(step < last)
    def _():
        op_ref[...] = _ffn_rows(xp_ref[...], g_ref[...], wg_ref, wu_ref, wd_ref)

    @pl.when(step == last)
    def _():
        os_ref[...] = _ffn_rows(xs_ref[...], g_ref[...], wg_ref, wu_ref, wd_ref)


def _ffn(xp, xs, g, wg, wu, wd, layer, tm):
    m, ms = xp.shape[0], xs.shape[0]
    n_p = m // tm
    row_p = pl.BlockSpec((tm, D_MODEL), lambda i: (jnp.minimum(i, n_p - 1), 0))
    return pl.pallas_call(
        _ffn_body, grid=(n_p + 1,),
        in_specs=[row_p, _const_spec((ms, D_MODEL)), _const_spec((1, D_MODEL)),
                  _layer_spec((D_MODEL, D_FF), layer), _layer_spec((D_MODEL, D_FF), layer),
                  _layer_spec((D_FF, D_MODEL), layer)],
        out_specs=[row_p, pl.BlockSpec((ms, D_MODEL), lambda i: (0, 0))],
        out_shape=[jax.ShapeDtypeStruct((m, D_MODEL), F32),
                   jax.ShapeDtypeStruct((ms, D_MODEL), F32)],
        compiler_params=_params(1), name="ffn")(xp, xs, g, wg, wu, wd)


def _conv_project(x, g, win_ref):
    bcu = _dot(_rms(x, g), win_ref[...])
    return bcu[:, :D_MODEL], bcu[:, D_MODEL:2 * D_MODEL] * bcu[:, 2 * D_MODEL:]


def _conv_body(xp_ref, stp_ref, xs_ref, sts_ref, g_ref, win_ref, cw_ref, wout_ref,
               op_ref, nstp_ref, os_ref, nsts_ref, ext_ref, *, tiles_per_seq):
    step = pl.program_id(0)
    last = pl.num_programs(0) - 1
    cw = cw_ref[...]

    @pl.when(step < last)
    def _():
        tm = xp_ref.shape[0]
        lead = SUBLANES - (CONV_W - 1)

        @pl.when(lax.rem(step, tiles_per_seq) == 0)
        def _():
            ext_ref[lead:SUBLANES, :] = stp_ref[...]

        x = xp_ref[...]
        b, cu = _conv_project(x, g_ref[...], win_ref)
        ext_ref[SUBLANES:SUBLANES + tm, :] = cu
        conv = (cw[0:1] * ext_ref[lead:lead + tm, :]
                + cw[1:2] * ext_ref[lead + 1:lead + 1 + tm, :] + cw[2:3] * cu)
        op_ref[...] = x + _dot(b * conv, wout_ref[...])
        tail = ext_ref[lead + tm:SUBLANES + tm, :]
        ext_ref[lead:SUBLANES, :] = tail
        nstp_ref[...] = tail

    @pl.when(step == last)
    def _():
        x = xs_ref[...]
        b, cu = _conv_project(x, g_ref[...], win_ref)
        s1 = sts_ref[:, D_MODEL:]
        conv = cw[0:1] * sts_ref[:, :D_MODEL] + cw[1:2] * s1 + cw[2:3] * cu
        os_ref[...] = x + _dot(b * conv, wout_ref[...])
        nsts_ref[:, :D_MODEL] = s1
        nsts_ref[:, D_MODEL:] = cu


def _conv(xp, state_p, xs, state_s, g, w_in, cw, w_out, layer, tm):
    nb, s, _ = xp.shape
    ms = xs.shape[0]
    nt = s // tm
    n_p = nb * nt

    def tile(i):
        return jnp.minimum(i, n_p - 1)

    row = pl.BlockSpec((None, tm, D_MODEL), lambda i: (tile(i) // nt, tile(i) % nt, 0))
    st = pl.BlockSpec((None, CONV_W - 1, D_MODEL), lambda i: (tile(i) // nt, 0, 0))
    return pl.pallas_call(
        functools.partial(_conv_body, tiles_per_seq=nt), grid=(n_p + 1,),
        in_specs=[row, st, _const_spec((ms, D_MODEL)), _const_spec((ms, 2 * D_MODEL)),
                  _const_spec((1, D_MODEL)), _layer_spec((D_MODEL, 3 * D_MODEL), layer),
                  _const_spec((CONV_W, D_MODEL)), _layer_spec((D_MODEL, D_MODEL), layer)],
        out_specs=[row, st, pl.BlockSpec((ms, D_MODEL), lambda i: (0, 0)),
                   pl.BlockSpec((ms, 2 * D_MODEL), lambda i: (0, 0))],
        out_shape=[jax.ShapeDtypeStruct(xp.shape, F32),
                   jax.ShapeDtypeStruct((nb, CONV_W - 1, D_MODEL), F32),
                   jax.ShapeDtypeStruct((ms, D_MODEL), F32),
                   jax.ShapeDtypeStruct((ms, 2 * D_MODEL), F32)],
        scratch_shapes=[pltpu.VMEM((tm + SUBLANES, D_MODEL), F32)],
        compiler_params=_params(1), name="conv")(xp, state_p, xs, state_s, g, w_in, cw, w_out)


def _kv_rows(x, g, w, gk, cos, sin_lo, sin_hi):
    kv = _dot(_rms(x, g), w)
    k = _head_norm_rope(kv[:, :KV_DIM], gk, cos, sin_lo, sin_hi)
    return k, kv[:, KV_DIM:]


def _kv_prompt_body(x_ref, g_ref, w_ref, gk_ref, cos_ref, slo_ref, shi_ref,
                    k_ref, vt_ref, klast_ref, vlast_ref):
    k, v = _kv_rows(x_ref[...], g_ref[...], w_ref[...], gk_ref[...],
                    cos_ref[...], slo_ref[...], shi_ref[...])
    k_ref[...] = k.astype(BF16)
    vt_ref[...] = v.T.astype(BF16)
    tm = k.shape[0]
    klast_ref[...] = k[tm - WINDOW:]
    vlast_ref[...] = v[tm - WINDOW:]


def _kv_prompt(x, g, w, gk, tables, tm):
    nb, s, _ = x.shape
    row = pl.BlockSpec((None, tm, D_MODEL), lambda b, t: (b, t, 0))
    tab = pl.BlockSpec((tm, LANES), lambda b, t: (t, 0))
    out = pl.BlockSpec((None, tm, KV_DIM), lambda b, t: (b, t, 0))
    out_t = pl.BlockSpec((None, KV_DIM, tm), lambda b, t: (b, 0, t))
    last = pl.BlockSpec((None, WINDOW, KV_DIM), lambda b, t: (b, 0, 0))
    return pl.pallas_call(
        _kv_prompt_body, grid=(nb, s // tm),
        in_specs=[row, _const_spec((1, D_MODEL)), _const_spec((D_MODEL, 2 * KV_DIM)),
                  _const_spec((1, KV_DIM)), tab, tab, tab],
        out_specs=[out, out_t, last, last],
        out_shape=[jax.ShapeDtypeStruct((nb, s, KV_DIM), BF16),
                   jax.ShapeDtypeStruct((nb, KV_DIM, s), BF16)]
        + [jax.ShapeDtypeStruct((nb, WINDOW, KV_DIM), F32)] * 2,
        compiler_params=_params(2), name="kv_prompt")(x, g, w, gk, *tables)


def _kv_sample_body(x_ref, g_ref, w_ref, gk_ref, cos_ref, slo_ref, shi_ref, k_ref, v_ref):
    k_ref[...], v_ref[...] = _kv_rows(x_ref[...], g_ref[...], w_ref[...], gk_ref[...],
                                      cos_ref[...], slo_ref[...], shi_ref[...])


def _kv_sample(x, g, w, gk, tables):
    m = x.shape[0]
    out = pl.BlockSpec((m, KV_DIM), lambda i: (0, 0))
    return pl.pallas_call(
        _kv_sample_body, grid=(1,),
        in_specs=[_const_spec((m, D_MODEL)), _const_spec((1, D_MODEL)),
                  _const_spec((D_MODEL, 2 * KV_DIM)), _const_spec((1, KV_DIM))]
        + [_const_spec((m, LANES))] * 3,
        out_specs=[out, out],
        out_shape=[jax.ShapeDtypeStruct((m, KV_DIM), F32)] * 2,
        compiler_params=_params(1), name="kv_sample")(x, g, w, gk, *tables)


def _queries(x, g, wq, gq, cos, sin_lo, sin_hi):
    q = _dot(_rms(x, g), wq)
    return _head_norm_rope(q, gq * (LOG2E / math.sqrt(HEAD_DIM)), cos, sin_lo, sin_hi)


def _attn_prompt_body(sinks_ref, x_ref, kc_ref, kp_ref, vtc_ref, vtp_ref, g_ref, wq_ref, gq_ref,
                      cos_ref, slo_ref, shi_ref, wo_ref, o_ref, q_scr, o_scr):
    tm = x_ref.shape[0]
    first_tile = pl.program_id(1) == 0
    x = x_ref[...]
    q_scr[...] = _queries(x, g_ref[...], wq_ref[...], gq_ref[...],
                          cos_ref[...], slo_ref[...], shi_ref[...]).astype(BF16)
    kcat = jnp.concatenate([kp_ref[...], kc_ref[...]], axis=0)
    vtcat = jnp.concatenate([vtp_ref[...], vtc_ref[...]], axis=1)

    seg = lax.broadcasted_iota(jnp.int32, (1, KV_DIM), 1) // HEAD_DIM
    kj = lax.broadcasted_iota(jnp.int32, (2 * WINDOW, WINDOW), 0)
    qi = lax.broadcasted_iota(jnp.int32, (2 * WINDOW, WINDOW), 1)
    rel = qi + WINDOW - kj
    band = (rel >= 0) & (rel < WINDOW)
    band0 = band & ((kj >= WINDOW) | jnp.logical_not(first_tile))
    bias = jnp.where(band, 0.0, -jnp.inf).astype(F32)
    bias0 = jnp.where(band0, 0.0, -jnp.inf).astype(F32)
    no_rows = jnp.zeros((HEAD_DIM, 2 * WINDOW), BF16)

    for i in range(tm // WINDOW):
        rows = slice(WINDOW * i, WINDOW * (i + 1))
        kw = kcat[WINDOW * i:WINDOW * (i + 2)]
        vtw = vtcat[:, WINDOW * i:WINDOW * (i + 2)]
        q4 = jnp.concatenate(
            [q_scr[rows, KV_DIM * g:KV_DIM * (g + 1)] for g in range(GROUP)], axis=0)
        bias4 = jnp.concatenate([bias0 if i == 0 else bias] * GROUP, axis=1)
        probs, vts, rdens = [], [], []
        for h in range(N_KV_HEADS):
            s = _dot_nt(jnp.where(seg == h, kw, jnp.zeros_like(kw)), q4) + bias4
            sink = jnp.concatenate(
                [jnp.full((1, WINDOW), sinks_ref[N_KV_HEADS * g + h] * LOG2E, F32)
                 for g in range(GROUP)], axis=1)
            m = jnp.maximum(jnp.max(s, axis=0, keepdims=True), sink)
            p = jnp.exp2(s - m)
            rdens.append(1.0 / (jnp.sum(p, axis=0, keepdims=True) + jnp.exp2(sink - m)))
            probs.append(p.astype(BF16))
            vts.append(jnp.concatenate(
                [vtw[HEAD_DIM * h:HEAD_DIM * (h + 1)] if hh == h else no_rows
                 for hh in range(N_KV_HEADS)], axis=0))
        o_t = _dot(jnp.concatenate(vts, axis=1), jnp.concatenate(probs, axis=0))
        o_t = jnp.concatenate(
            [o_t[HEAD_DIM * h:HEAD_DIM * (h + 1)] * rdens[h] for h in range(N_KV_HEADS)], axis=0)
        for g in range(GROUP):
            o_scr[rows, KV_DIM * g:KV_DIM * (g + 1)] = (
                o_t[:, WINDOW * g:WINDOW * (g + 1)].T)

    o_ref[...] = x + _dot(o_scr[...], wo_ref[...])


def _attn_prompt(x, k, vt, g, wq, gq, tables, sinks, wo, layer, tm):
    nb, s, _ = x.shape
    per_tile = tm // WINDOW
    row = pl.BlockSpec((None, tm, D_MODEL), lambda b, t: (b, t, 0))
    cur = pl.BlockSpec((None, tm, KV_DIM), lambda b, t: (b, t, 0))
    prev = pl.BlockSpec((None, WINDOW, KV_DIM),
                        lambda b, t: (b, jnp.maximum(t * per_tile - 1, 0), 0))
    cur_t = pl.BlockSpec((None, KV_DIM, tm), lambda b, t: (b, 0, t))
    prev_t = pl.BlockSpec((None, KV_DIM, WINDOW),
                          lambda b, t: (b, 0, jnp.maximum(t * per_tile - 1, 0)))
    tab = pl.BlockSpec((tm, LANES), lambda b, t: (t, 0))
    return pl.pallas_call(
        _attn_prompt_body, grid=(nb, s // tm),
        in_specs=[pl.BlockSpec(memory_space=pltpu.SMEM), row, cur, prev, cur_t, prev_t,
                  _const_spec((1, D_MODEL)), _layer_spec((D_MODEL, D_MODEL), layer),
                  _const_spec((1, D_MODEL)), tab, tab, tab, _layer_spec((D_MODEL, D_MODEL), layer)],
        out_specs=row,
        out_shape=jax.ShapeDtypeStruct(x.shape, F32),
        scratch_shapes=[pltpu.VMEM((tm, D_MODEL), BF16), pltpu.VMEM((tm, D_MODEL), F32)],
        compiler_params=_params(2), name="attn_prompt")(
            sinks, x, k, k, vt, vt, g, wq, gq, *tables, wo)


def _attn_sample_body(sinks_ref, x_ref, ck_ref, cv_ref, kn_ref, vn_ref, g_ref, wq_ref, gq_ref,
                      cos_ref, slo_ref, shi_ref, wo_ref, o_ref, *rest, shift_cache):
    if shift_cache:
        nk_ref, nv_ref, q_scr, s_scr, p_scr, o_scr = rest
    else:
        q_scr, s_scr, p_scr, o_scr = rest
    chunk = x_ref.shape[0]
    x = x_ref[...]
    q_scr[...] = _queries(x, g_ref[...], wq_ref[...], gq_ref[...],
                          cos_ref[...], slo_ref[...], shi_ref[...])
    w_buf = ck_ref.shape[1]
    seg_rows = (lax.broadcasted_iota(jnp.int32, (N_KV_HEADS, KV_DIM), 1) // HEAD_DIM
                == lax.broadcasted_iota(jnp.int32, (N_KV_HEADS, KV_DIM), 0))
    expired = lax.broadcasted_iota(jnp.int32, (w_buf, 1), 0) == 0

    def scores(b, carry):
        kb = jnp.where(expired, kn_ref[pl.ds(b, 1), :], ck_ref[b]).astype(BF16)
        qrow = q_scr[pl.ds(b, 1), :]
        qrows = jnp.concatenate(
            [jnp.where(seg_rows, jnp.broadcast_to(qrow[:, KV_DIM * g:KV_DIM * (g + 1)],
                                                  (N_KV_HEADS, KV_DIM)), 0.0)
             for g in range(GROUP)], axis=0)
        s_scr[pl.ds(pl.multiple_of(b * N_HEADS, N_HEADS), N_HEADS), :] = _dot_nt(
            qrows.astype(BF16), kb)
        if shift_cache:
            nk_ref[b, 0:w_buf - 1, :] = ck_ref[b, 1:w_buf, :]
            nk_ref[b, w_buf - 1:w_buf, :] = kn_ref[pl.ds(b, 1), :]
        return carry

    lax.fori_loop(0, chunk, scores, 0, unroll=SAMPLE_UNROLL)

    s = s_scr[...]
    sink = jnp.concatenate(
        [jnp.full((1, 1), sinks_ref[i] * LOG2E, F32) for i in range(N_HEADS)] * chunk, axis=0)
    m = jnp.maximum(jnp.max(s, axis=-1, keepdims=True), sink)
    p = jnp.exp2(s - m)
    den = jnp.sum(p, axis=-1, keepdims=True) + jnp.exp2(sink - m)
    p_scr[...] = (p * (1.0 / den)).astype(BF16)

    def outputs(b, carry):
        vb = jnp.where(expired, vn_ref[pl.ds(b, 1), :], cv_ref[b]).astype(BF16)
        r = _dot(p_scr[pl.ds(pl.multiple_of(b * N_HEADS, N_HEADS), N_HEADS), :], vb)
        o_scr[pl.ds(b, 1), :] = jnp.concatenate(
            [jnp.sum(jnp.where(seg_rows, r[N_KV_HEADS * g:N_KV_HEADS * (g + 1)], 0.0),
                     axis=0, keepdims=True) for g in range(GROUP)], axis=1)
        if shift_cache:
            nv_ref[b, 0:w_buf - 1, :] = cv_ref[b, 1:w_buf, :]
            nv_ref[b, w_buf - 1:w_buf, :] = vn_ref[pl.ds(b, 1), :]
        return carry

    lax.fori_loop(0, chunk, outputs, 0, unroll=SAMPLE_UNROLL)
    o_ref[...] = x + _dot(o_scr[...], wo_ref[...])


def _attn_sample(x, ck, cv, kn, vn, g, wq, gq, tables, sinks, wo, layer, chunk, shift_cache):
    m = x.shape[0]
    w_buf = ck.shape[1]
    row = pl.BlockSpec((chunk, D_MODEL), lambda i: (i, 0))
    tab = pl.BlockSpec((chunk, LANES), lambda i: (i, 0))
    new = pl.BlockSpec((chunk, KV_DIM), lambda i: (i, 0))
    cache = pl.BlockSpec((chunk, w_buf, KV_DIM), lambda i: (i, 0, 0))
    out_specs, out_shape = [row], [jax.ShapeDtypeStruct(x.shape, F32)]
    if shift_cache:
        out_specs += [cache, cache]
        out_shape += [jax.ShapeDtypeStruct(ck.shape, F32)] * 2
    return pl.pallas_call(
        functools.partial(_attn_sample_body, shift_cache=shift_cache), grid=(m // chunk,),
        in_specs=[pl.BlockSpec(memory_space=pltpu.SMEM), row, cache, cache, new, new,
                  _const_spec((1, D_MODEL)), _layer_spec((D_MODEL, D_MODEL), layer),
                  _const_spec((1, D_MODEL)), tab, tab, tab, _layer_spec((D_MODEL, D_MODEL), layer)],
        out_specs=out_specs, out_shape=out_shape,
        scratch_shapes=[pltpu.VMEM((chunk, D_MODEL), F32),
                        pltpu.VMEM((chunk * N_HEADS, w_buf), F32),
                        pltpu.VMEM((chunk * N_HEADS, w_buf), BF16),
                        pltpu.VMEM((chunk, D_MODEL), F32)],
        compiler_params=_params(1), name="attn_sample")(
            sinks, x, ck, cv, kn, vn, g, wq, gq, *tables, wo)


def _rope_tables(pos):
    half = ROT_DIM // 2
    inv_freq = ROPE_THETA ** (-jnp.arange(0, ROT_DIM, 2, dtype=F32) / ROT_DIM)
    ang = pos.astype(F32)[:, None] * inv_freq[None, :]
    cos, sin = jnp.cos(ang), jnp.sin(ang)
    n = pos.shape[0]
    rest = HEAD_DIM - ROT_DIM
    cos_h = jnp.concatenate([cos, cos, jnp.ones((n, rest), F32)], axis=1)
    sin_lo = jnp.concatenate([-sin, jnp.zeros((n, half + rest), F32)], axis=1)
    sin_hi = jnp.concatenate([jnp.zeros((n, half), F32), sin, jnp.zeros((n, rest), F32)], axis=1)
    reps = LANES // HEAD_DIM
    return tuple(jnp.tile(t, (1, reps)) for t in (cos_h, sin_lo, sin_hi))


def kernel(x_prompt, x_sample, state_conv, cache_k, cache_v, g_ffn1, w_ffn1_gate, w_ffn1_up,
           w_ffn1_down, g_mix, g_ffn2, w_ffn2_gate, w_ffn2_up, w_ffn2_down, w_in_a, conv_w,
           w_out_a, g_kv, w_kv, g_knorm, w_q, g_qnorm, sinks, w_o):
    nb, seq, _ = x_prompt.shape
    nd, dec_seq, _ = x_sample.shape
    assert dec_seq == 1 and cache_k.shape[1] == WINDOW
    depth = g_ffn1.shape[0]
    n_a = w_in_a.shape[0]
    past_len = seq
    w_buf = cache_k.shape[1]

    n_b = w_q.shape[0]
    ffn1 = (w_ffn1_gate, w_ffn1_up, w_ffn1_down)
    ffn2 = (w_ffn2_gate, w_ffn2_up, w_ffn2_down)
    w_in, w_out, wkv = w_in_a, w_out_a, w_kv
    wq = (w_q.reshape(n_b, D_MODEL, N_KV_HEADS, GROUP, HEAD_DIM).transpose(0, 1, 3, 2, 4)
          .reshape(n_b, D_MODEL, D_MODEL))
    wo = (w_o.reshape(n_b, N_KV_HEADS, GROUP, HEAD_DIM, D_MODEL).transpose(0, 2, 1, 3, 4)
          .reshape(n_b, D_MODEL, D_MODEL))
    sinks_gm = sinks.reshape(n_b, N_KV_HEADS, GROUP).transpose(0, 2, 1).reshape(n_b, N_HEADS)
    tab_p = _rope_tables(jnp.arange(seq, dtype=jnp.int32))
    tab_s = _rope_tables(jnp.full((nd,), past_len, jnp.int32))
    gk = jnp.tile(g_knorm, N_KV_HEADS)[None]

    xp = x_prompt
    xs = x_sample.reshape(nd, D_MODEL)
    ck = cache_k.reshape(nd, w_buf, KV_DIM)
    cv = cache_v.reshape(nd, w_buf, KV_DIM)
    conv_p, conv_s = [], []
    kp = vtp = k_last = v_last = kn = vn = new_ck = new_cv = None

    def ffn_both(xp, xs, g, weights, layer):
        xp, xs = _ffn(xp.reshape(nb * seq, D_MODEL), xs, g[None], *weights, layer, TM_FFN)
        return xp.reshape(nb, seq, D_MODEL), xs

    for i in range(depth):
        if i == n_a:
            kp, vtp, k_last, v_last = _kv_prompt(xp, g_kv[None], wkv, gk, tab_p, TM_KV)
            kn, vn = _kv_sample(xs, g_kv[None], wkv, gk, tab_s)
        xp, xs = ffn_both(xp, xs, g_ffn1[i], ffn1, i)
        gm = g_mix[i][None]
        if i < n_a:
            xp, st_p, xs, st_s = _conv(
                xp, jnp.zeros((nb, CONV_W - 1, D_MODEL), F32), xs,
                state_conv[i].reshape(nd, (CONV_W - 1) * D_MODEL), gm, w_in, conv_w[i], w_out,
                i, TM_CONV)
            conv_p.append(st_p)
            conv_s.append(st_s.reshape(nd, CONV_W - 1, D_MODEL))
        else:
            j = i - n_a
            gq = jnp.tile(g_qnorm[j], N_HEADS)[None]
            xp = _attn_prompt(xp, kp, vtp, gm, wq, gq, tab_p, sinks_gm[j], wo, j, TM_ATTN)
            outs = _attn_sample(xs, ck, cv, kn, vn, gm, wq, gq, tab_s, sinks_gm[j], wo, j,
                                SAMPLE_CHUNK, shift_cache=(j == 0))
            xs = outs[0]
            if j == 0:
                new_ck, new_cv = outs[1:]
        xp, xs = ffn_both(xp, xs, g_ffn2[i], ffn2, i)

    kv_shape = (w_buf, N_KV_HEADS, HEAD_DIM)
    return (xp, xs.reshape(nd, 1, D_MODEL), jnp.stack(conv_p),
            k_last.reshape(nb, *kv_shape), v_last.reshape(nb, *kv_shape),
            jnp.stack(conv_s), new_ck.reshape(nd, *kv_shape), new_cv.reshape(nd, *kv_shape))
```
